```python
import math
import jax
import jax.numpy as jnp
from jax import lax
import numpy as np

D_MODEL = 2048
BATCH = 2
SEQ = 4096
DEPTH = 4
DEC_BATCH = 32
DEC_SEQ = 8
PAST_LEN = 16384
PAGE_SIZE = 128

HEAD_DIM = 64
MIX_WIDTH = D_MODEL
N_HEADS_A = MIX_WIDTH // (2 * HEAD_DIM)
N_KV_A = max(1, N_HEADS_A // 8)
N_HEADS_B = MIX_WIDTH // HEAD_DIM - N_HEADS_A
WINDOW_A = 128
DILATED = ((128, 1), (512, 4), (2048, 16))
WINDOW_B = 2048
BLOCK = 128
NUM_BUCKETS = 32
MAX_DISTANCE = 2048
ATTN_SCALE = HEAD_DIM ** -0.5
NEG_INF = -1e30
D_FF = ((8 * D_MODEL // 3 + 255) // 256) * 256
N_EXPERTS = 8
TOP_K = 2
D_FF_EXPERT = 7 * D_MODEL // 2
MOE_BLOCK = 256
RMS_EPS = 1e-5
N_DENSE = (DEPTH + 1) // 2
N_MOE = DEPTH // 2
QKV_DIM = (N_HEADS_A + 2 * N_KV_A + 3 * N_HEADS_B) * HEAD_DIM

kernel_name = 'hymba_swa_sink_dilated_moe_step'


def _rmsnorm(x, g):
    x32 = x.astype(jnp.float32)
    y = x32 * lax.rsqrt(jnp.mean(x32 * x32, axis=-1, keepdims=True) + RMS_EPS)
    return (y * g.astype(jnp.float32)).astype(x.dtype)


def _t5_bucket(dist):
    n = jnp.maximum(dist, 0)
    max_exact = NUM_BUCKETS // 2
    ratio = jnp.log(jnp.maximum(n, max_exact).astype(jnp.float32) / max_exact) / math.log(MAX_DISTANCE / max_exact)
    large = max_exact + (ratio * (NUM_BUCKETS - max_exact)).astype(jnp.int32)
    large = jnp.minimum(large, NUM_BUCKETS - 1)
    return jnp.where(n < max_exact, n, large)


def _attend(q, k, v, qpos, kpos, max_dist, bias_tab, sink):
    b, n, tq, h, hd = q.shape
    tk, kvh = k.shape[2], k.shape[3]
    g = h // kvh
    qg = q.reshape(b, n, tq, kvh, g, hd)
    s = jnp.einsum('bnqkgd,bnskd->bnkgqs', qg, k, preferred_element_type=jnp.float32) * ATTN_SCALE
    dist = qpos[:, :, None] - kpos[:, None, :]
    valid = (dist >= 0) & (dist <= max_dist) & (kpos[:, None, :] >= 0)
    bias = bias_tab.astype(jnp.float32)[_t5_bucket(dist)]
    bias = bias.reshape(n, tq, tk, kvh, g).transpose(0, 3, 4, 1, 2)
    s = jnp.where(valid[:, None, None], s + bias, NEG_INF)
    lse = jax.nn.logsumexp(s, axis=-1)
    if sink is not None:
        lse = jnp.logaddexp(lse, sink.astype(jnp.float32).reshape(kvh, g, 1))
    p = jnp.exp(s - lse[..., None]).astype(v.dtype)
    o = jnp.einsum('bnkgqs,bnskd->bnqkgd', p, v).reshape(b, n, tq, h, hd)
    lse = lse.transpose(0, 1, 4, 2, 3).reshape(b, n, tq, h)
    return o, lse


def _banded_attend(q, k, v, step, max_dist, bias_tab, sink):
    bx, length, h, hd = q.shape
    kvh = k.shape[2]
    nb = length // BLOCK
    qb = q.reshape(bx, nb, BLOCK, h, hd)

    def band(t):
        tb = t.reshape(bx, nb, BLOCK, kvh, hd)
        prev = jnp.pad(tb, ((0, 0), (1, 0), (0, 0), (0, 0), (0, 0)))[:, :-1]
        return jnp.concatenate([prev, tb], axis=2)

    start = jnp.arange(nb)[:, None] * BLOCK
    qpos = (start + jnp.arange(BLOCK)[None, :]) * step
    kpos = (start - BLOCK + jnp.arange(2 * BLOCK)[None, :]) * step
    o, lse = _attend(qb, band(k), band(v), qpos, kpos, max_dist, bias_tab, sink)
    return o.reshape(bx, length, h, hd), lse.reshape(bx, length, h)


def _combine_dilated(outs, lses, dtype):
    wts = jax.nn.softmax(jnp.stack(lses, axis=0), axis=0)
    o = jnp.sum(wts[..., None] * jnp.stack(outs, axis=0).astype(jnp.float32), axis=0)
    return o.astype(dtype)


def _dilated_prompt(q, k, v, bias_tab):
    b, s, h, hd = q.shape
    outs, lses = [], []
    for w, r in DILATED:
        span = r * BLOCK
        s_pad = -(-s // span) * span
        m = s_pad // r

        def to_sub(t):
            t = jnp.pad(t, ((0, 0), (0, s_pad - s), (0, 0), (0, 0)))
            return t.reshape(b, m, r, h, hd).transpose(0, 2, 1, 3, 4).reshape(b * r, m, h, hd)

        o, lse = _banded_attend(to_sub(q), to_sub(k), to_sub(v), r, w, bias_tab, None)
        outs.append(o.reshape(b, r, m, h, hd).transpose(0, 2, 1, 3, 4).reshape(b, s_pad, h, hd)[:, :s])
        lses.append(lse.reshape(b, r, m, h).transpose(0, 2, 1, 3).reshape(b, s_pad, h)[:, :s])
    return _combine_dilated(outs, lses, q.dtype)


def _window_sample(q, k, v, buf_k, buf_v, bias_tab, sink):
    t = q.shape[1]
    lw = buf_k.shape[1]
    k_all = jnp.concatenate([buf_k, k], axis=1)
    v_all = jnp.concatenate([buf_v, v], axis=1)
    qpos = (PAST_LEN + jnp.arange(t))[None, :]
    kpos = (PAST_LEN - lw + jnp.arange(lw + t))[None, :]
    o, _ = _attend(q[:, None], k_all[:, None], v_all[:, None], qpos, kpos, WINDOW_A - 1, bias_tab, sink)
    keep = min(WINDOW_A, lw + t)
    return o[:, 0], k_all[:, -keep:], v_all[:, -keep:]


def _dilated_sample(q, k, v, buf_k, buf_v, bias_tab):
    t = q.shape[1]
    lb = buf_k.shape[1]
    k_all = jnp.concatenate([buf_k, k], axis=1)
    v_all = jnp.concatenate([buf_v, v], axis=1)
    qpos = PAST_LEN + jnp.arange(t)
    base = PAST_LEN - lb
    outs, lses = [], []
    for w, r in DILATED:
        kp = qpos[:, None] - r * jnp.arange(w // r + 1)[None, :]
        idx = kp - base
        kpos = jnp.where(idx >= 0, kp, -1)
        idx = jnp.maximum(idx, 0)
        kg = k_all[:, idx]
        vg = v_all[:, idx]
        o, lse = _attend(q[:, :, None], kg, vg, qpos[:, None], kpos, w, bias_tab, None)
        outs.append(o[:, :, 0])
        lses.append(lse[:, :, 0])
    keep = min(WINDOW_B, lb + t)
    return _combine_dilated(outs, lses, q.dtype), k_all[:, -keep:], v_all[:, -keep:]


def _project(h, w_in):
    proj = h @ w_in
    lead = proj.shape[:-1]
    heads = (N_HEADS_A, N_KV_A, N_KV_A, N_HEADS_B, N_HEADS_B, N_HEADS_B)
    cuts = [int(c) * HEAD_DIM for c in np.cumsum(heads)[:-1]]
    parts = jnp.split(proj, cuts, axis=-1)
    return [p.reshape(*lead, nh, HEAD_DIM) for p, nh in zip(parts, heads)]


def _merge(oa, ob, g_a, g_b, w_out):
    lead = oa.shape[:-2]
    ya = _rmsnorm(oa.reshape(*lead, N_HEADS_A * HEAD_DIM), g_a)
    yb = _rmsnorm(ob.reshape(*lead, N_HEADS_B * HEAD_DIM), g_b)
    return jnp.concatenate([ya, yb], axis=-1) @ w_out


def _swiglu(h, wg, wu, wd):
    return (jax.nn.silu(h @ wg) * (h @ wu)) @ wd


def _moe(h, router, wg, wu, wd):
    shp = h.shape
    xt = h.reshape(-1, shp[-1])
    t = xt.shape[0]
    logits = jnp.einsum('td,de->te', xt, router, preferred_element_type=jnp.float32)
    top_v, top_i = lax.top_k(logits, TOP_K)
    gates = jax.nn.softmax(top_v, axis=-1)
    flat_e = top_i.reshape(-1)
    flat_t = jnp.repeat(jnp.arange(t), TOP_K)
    flat_g = gates.reshape(-1)
    order = jnp.argsort(flat_e)
    se, st, sg = flat_e[order], flat_t[order], flat_g[order]
    counts = jnp.bincount(flat_e, length=N_EXPERTS)
    starts = jnp.cumsum(counts) - counts
    padded = (counts + MOE_BLOCK - 1) // MOE_BLOCK * MOE_BLOCK
    pstarts = jnp.cumsum(padded) - padded
    dest = pstarts[se] + (jnp.arange(TOP_K * t) - starts[se])
    n_rows = (TOP_K * t + MOE_BLOCK - 1) // MOE_BLOCK * MOE_BLOCK + N_EXPERTS * MOE_BLOCK
    n_blk = n_rows // MOE_BLOCK
    row_tok = jnp.full((n_rows,), t, jnp.int32).at[dest].set(st)
    row_gate = jnp.zeros((n_rows,), jnp.float32).at[dest].set(sg)
    blk_start = jnp.arange(n_blk) * MOE_BLOCK
    blk_e = jnp.minimum(jnp.sum(blk_start[:, None] >= (pstarts + padded)[None, :], axis=1), N_EXPERTS - 1)
    xpad = jnp.concatenate([xt, jnp.zeros((1, xt.shape[1]), xt.dtype)], axis=0)
    xb = xpad[row_tok].reshape(n_blk, MOE_BLOCK, xt.shape[1])

    def expert_block(args):
        xblk, e = args
        return _swiglu(xblk, wg[e], wu[e], wd[e])

    yb = lax.map(expert_block, (xb, blk_e)).reshape(n_rows, xt.shape[1])
    y = jnp.zeros((t + 1, xt.shape[1]), jnp.float32).at[row_tok].add(yb.astype(jnp.float32) * row_gate[:, None])[:t]
    return y.astype(h.dtype).reshape(shp)


def _channel(h, l, w_gate_d, w_up_d, w_down_d, router, w_gate_e, w_up_e, w_down_e):
    i = l // 2
    if l % 2 == 0:
        return _swiglu(h, w_gate_d[i], w_up_d[i], w_down_d[i])
    return _moe(h, router[i], w_gate_e[i], w_up_e[i], w_down_e[i])


def setup_inputs(seed: int = 0) -> dict:
    key = jax.random.key(seed)
    ks = jax.random.split(key, 24)

    def nrm(k, shape, scale):
        return jax.random.normal(k, shape, jnp.float32) * scale

    la = min(WINDOW_A, PAST_LEN)
    lb = min(WINDOW_B, PAST_LEN)
    return {
        'x_prompt': nrm(ks[0], (BATCH, SEQ, D_MODEL), 1.0),
        'x_sample': nrm(ks[1], (DEC_BATCH, DEC_SEQ, D_MODEL), 1.0),
        'cache_a_k': nrm(ks[2], (DEPTH, DEC_BATCH, la, N_KV_A, HEAD_DIM), 1.0),
        'cache_a_v': nrm(ks[3], (DEPTH, DEC_BATCH, la, N_KV_A, HEAD_DIM), 1.0),
        'cache_b_k': nrm(ks[4], (DEPTH, DEC_BATCH, lb, N_HEADS_B, HEAD_DIM), 1.0),
        'cache_b_v': nrm(ks[5], (DEPTH, DEC_BATCH, lb, N_HEADS_B, HEAD_DIM), 1.0),
        'g_mix_in': 1.0 + nrm(ks[6], (DEPTH, D_MODEL), 0.02),
        'w_in': nrm(ks[7], (DEPTH, D_MODEL, QKV_DIM), D_MODEL ** -0.5),
        'sinks': nrm(ks[8], (DEPTH, N_HEADS_A), 0.5),
        'rel_bias': nrm(ks[9], (NUM_BUCKETS, N_HEADS_A + N_HEADS_B), 0.5),
        'g_out_a': 1.0 + nrm(ks[10], (DEPTH, N_HEADS_A * HEAD_DIM), 0.02),
        'g_out_b': 1.0 + nrm(ks[11], (DEPTH, N_HEADS_B * HEAD_DIM), 0.02),
        'w_out': nrm(ks[12], (DEPTH, MIX_WIDTH, D_MODEL), MIX_WIDTH ** -0.5),
        'g_ffn': 1.0 + nrm(ks[13], (DEPTH, D_MODEL), 0.02),
        'w_gate_d': nrm(ks[14], (N_DENSE, D_MODEL, D_FF), D_MODEL ** -0.5),
        'w_up_d': nrm(ks[15], (N_DENSE, D_MODEL, D_FF), D_MODEL ** -0.5),
        'w_down_d': nrm(ks[16], (N_DENSE, D_FF, D_MODEL), D_FF ** -0.5),
        'router': nrm(ks[17], (N_MOE, D_MODEL, N_EXPERTS), D_MODEL ** -0.5),
        'w_gate_e': nrm(ks[18], (N_MOE, N_EXPERTS, D_MODEL, D_FF_EXPERT), D_MODEL ** -0.5),
        'w_up_e': nrm(ks[19], (N_MOE, N_EXPERTS, D_MODEL, D_FF_EXPERT), D_MODEL ** -0.5),
        'w_down_e': nrm(ks[20], (N_MOE, N_EXPERTS, D_FF_EXPERT, D_MODEL), D_FF_EXPERT ** -0.5),
        'g_final': 1.0 + nrm(ks[21], (D_MODEL,), 0.02),
    }


def reference(x_prompt, x_sample, cache_a_k, cache_a_v, cache_b_k, cache_b_v, g_mix_in, w_in, sinks, rel_bias, g_out_a, g_out_b, w_out, g_ffn, w_gate_d, w_up_d, w_down_d, router, w_gate_e, w_up_e, w_down_e, g_final):
    tab_a = rel_bias[:, :N_HEADS_A]
    tab_b = rel_bias[:, N_HEADS_A:]
    s = x_prompt.shape[1]
    xp, xs = x_prompt, x_sample
    ak_p, av_p, bk_p, bv_p = [], [], [], []
    ak_s, av_s, bk_s, bv_s = [], [], [], []
    for l in range(DEPTH):
        qa, ka, va, qb, kb, vb = _project(_rmsnorm(xp, g_mix_in[l]), w_in[l])
        oa, _ = _banded_attend(qa, ka, va, 1, WINDOW_A - 1, tab_a, sinks[l])
        ob = _dilated_prompt(qb, kb, vb, tab_b)
        xp = xp + _merge(oa, ob, g_out_a[l], g_out_b[l], w_out[l])
        xp = xp + _channel(_rmsnorm(xp, g_ffn[l]), l, w_gate_d, w_up_d, w_down_d, router, w_gate_e, w_up_e, w_down_e)
        ak_p.append(ka[:, -min(WINDOW_A, s):])
        av_p.append(va[:, -min(WINDOW_A, s):])
        bk_p.append(kb[:, -min(WINDOW_B, s):])
        bv_p.append(vb[:, -min(WINDOW_B, s):])
        qa, ka, va, qb, kb, vb = _project(_rmsnorm(xs, g_mix_in[l]), w_in[l])
        oa, nak, nav = _window_sample(qa, ka, va, cache_a_k[l], cache_a_v[l], tab_a, sinks[l])
        ob, nbk, nbv = _dilated_sample(qb, kb, vb, cache_b_k[l], cache_b_v[l], tab_b)
        xs = xs + _merge(oa, ob, g_out_a[l], g_out_b[l], w_out[l])
        xs = xs + _channel(_rmsnorm(xs, g_ffn[l]), l, w_gate_d, w_up_d, w_down_d, router, w_gate_e, w_up_e, w_down_e)
        ak_s.append(nak)
        av_s.append(nav)
        bk_s.append(nbk)
        bv_s.append(nbv)
    y_prompt = _rmsnorm(xp, g_final)
    y_sample = _rmsnorm(xs, g_final)
    return (y_prompt, y_sample, jnp.stack(ak_p), jnp.stack(av_p), jnp.stack(bk_p), jnp.stack(bv_p), jnp.stack(ak_s), jnp.stack(av_s), jnp.stack(bk_s), jnp.stack(bv_s))
```

```python
import functools
import math

import jax
import jax.numpy as jnp
from jax import lax
from jax.experimental import pallas as pl
from jax.experimental.pallas import tpu as pltpu

F32 = jnp.float32
BF16 = jnp.bfloat16

D_MODEL = 2048
BATCH = 2
SEQ = 4096
DEPTH = 4
DEC_BATCH = 32
DEC_SEQ = 8
PAST_LEN = 16384
HEAD_DIM = 64
N_HEADS_A = 16
N_KV_A = 2
N_HEADS_B = 16
WINDOW_A = 128
DILATED = ((128, 1), (512, 4), (2048, 16))
WINDOW_B = 2048
BLOCK = 128
NUM_BUCKETS = 32
MAX_DISTANCE = 2048
ATTN_SCALE = HEAD_DIM ** -0.5
NEG_INF = -1e30
D_FF = 5632
N_EXPERTS = 8
TOP_K = 2
D_FF_EXPERT = 7168
RMS_EPS = 1e-5

N_PROMPT = BATCH * SEQ
N_SAMPLE = DEC_BATCH * DEC_SEQ
N_TOK = N_PROMPT + N_SAMPLE
WIDTH_A = N_HEADS_A * HEAD_DIM
WIDTH_KVA = 2 * N_KV_A * HEAD_DIM
WIDTH_B = N_HEADS_B * HEAD_DIM
LANES = 128
ROW_CHUNKS = D_MODEL // LANES
MOE_ROWS = 256
N_ASSIGN = TOP_K * N_TOK
N_MOE_BLOCKS = N_ASSIGN // MOE_ROWS + N_EXPERTS
N_MOE_ROWS = N_MOE_BLOCKS * MOE_ROWS
VMEM_LIMIT = 56 * 1024 * 1024

TM = 768
TN_QKV = 256
CAST_ROWS = 256


def _params(sem):
    return pltpu.CompilerParams(dimension_semantics=sem, vmem_limit_bytes=VMEM_LIMIT)


def _cast_weight(w_ref, wbf_ref):
    steps = w_ref.shape[0] // CAST_ROWS

    def body(i, c):
        rows = pl.ds(pl.multiple_of(i * CAST_ROWS, CAST_ROWS), CAST_ROWS)
        wbf_ref[rows, :] = w_ref[rows, :].astype(BF16)
        return c

    lax.fori_loop(0, steps, body, 0)


def _rmsnorm_rows(x, g):
    ms = jnp.mean(x * x, axis=-1, keepdims=True)
    return x * lax.rsqrt(ms + RMS_EPS) * g


def _norm_kernel(x_ref, g_ref, o_ref):
    o_ref[...] = _rmsnorm_rows(x_ref[...], g_ref[...]).astype(o_ref.dtype)


def _norm_call(x, g, out_dtype, tm=256):
    rows, width = x.shape
    return pl.pallas_call(
        _norm_kernel,
        grid=(rows // tm,),
        in_specs=[pl.BlockSpec((tm, width), lambda i: (i, 0)),
                  pl.BlockSpec((1, width), lambda i: (0, 0))],
        out_specs=pl.BlockSpec((tm, width), lambda i: (i, 0)),
        out_shape=jax.ShapeDtypeStruct((rows, width), out_dtype),
        compiler_params=_params(("arbitrary",)),
        name="rmsnorm",
    )(x, g.reshape(1, width))


_QKV_TILES = (N_HEADS_A * HEAD_DIM + 2 * N_KV_A * HEAD_DIM + 3 * N_HEADS_B * HEAD_DIM) // TN_QKV
_QKV_SPLIT = ((0, 4), (4, 1), (5, 4), (9, 4), (13, 4))


def _qkv_kernel(x_ref, g_ref, w_ref, qa_ref, kva_ref, qb_ref, kb_ref, vb_ref, xn_ref):
    n = pl.program_id(1)

    @pl.when(n == 0)
    def _():
        xn_ref[...] = _rmsnorm_rows(x_ref[...], g_ref[...]).astype(BF16)

    y = jnp.dot(xn_ref[...], w_ref[...].astype(BF16), preferred_element_type=F32)
    for (first, count), o_ref in zip(_QKV_SPLIT, (qa_ref, kva_ref, qb_ref, kb_ref, vb_ref)):
        @pl.when((n >= first) & (n < first + count))
        def _(o_ref=o_ref):
            o_ref[...] = y


def _qkv_call(x, g, w_in, layer):
    def out_spec(first, count):
        return pl.BlockSpec((TM, TN_QKV), lambda m, n: (m, jnp.clip(n - first, 0, count - 1)))

    return pl.pallas_call(
        _qkv_kernel,
        grid=(N_TOK // TM, _QKV_TILES),
        in_specs=[pl.BlockSpec((TM, D_MODEL), lambda m, n: (m, 0)),
                  pl.BlockSpec((1, D_MODEL), lambda m, n: (0, 0)),
                  pl.BlockSpec((None, D_MODEL, TN_QKV), lambda m, n: (layer, 0, n))],
        out_specs=[out_spec(f, c) for f, c in _QKV_SPLIT],
        out_shape=[jax.ShapeDtypeStruct((N_TOK, c * TN_QKV), F32) for _, c in _QKV_SPLIT],
        scratch_shapes=[pltpu.VMEM((TM, D_MODEL), BF16)],
        compiler_params=_params(("arbitrary", "arbitrary")),
        name="qkv_proj",
    )(x, g.reshape(1, D_MODEL), w_in)


def _mm_res_kernel(a_ref, w_ref, r_ref, o_ref, wbf_ref):
    @pl.when(pl.program_id(1) == 0)
    def _():
        _cast_weight(w_ref, wbf_ref)

    o_ref[...] = r_ref[...] + jnp.dot(a_ref[...], wbf_ref[...], preferred_element_type=F32)


def _mm_res_call(a, w, layer, res, tm, tn, name):
    m_rows, k = a.shape
    n_cols = w.shape[2]
    return pl.pallas_call(
        _mm_res_kernel,
        grid=(n_cols // tn, m_rows // tm),
        in_specs=[pl.BlockSpec((tm, k), lambda n, m: (m, 0)),
                  pl.BlockSpec((None, k, tn), lambda n, m: (layer, 0, n)),
                  pl.BlockSpec((tm, tn), lambda n, m: (m, n))],
        out_specs=pl.BlockSpec((tm, tn), lambda n, m: (m, n)),
        out_shape=jax.ShapeDtypeStruct((m_rows, n_cols), F32),
        scratch_shapes=[pltpu.VMEM((k, tn), BF16)],
        compiler_params=_params(("arbitrary", "arbitrary")),
        name=name,
    )(a, w, res)


def _silu_mul(g, u):
    return g * (1.0 / (1.0 + jnp.exp(-g))) * u


def _swiglu_kernel(a_ref, wg_ref, wu_ref, o_ref, wgbf_ref, wubf_ref):
    @pl.when(pl.program_id(1) == 0)
    def _():
        _cast_weight(wg_ref, wgbf_ref)
        _cast_weight(wu_ref, wubf_ref)

    a = a_ref[...]
    g = jnp.dot(a, wgbf_ref[...], preferred_element_type=F32)
    u = jnp.dot(a, wubf_ref[...], preferred_element_type=F32)
    o_ref[...] = _silu_mul(g, u).astype(o_ref.dtype)


def _swiglu_call(a, wg, wu, idx, tm=TM, tn=512):
    m_rows, k = a.shape
    n_cols = wg.shape[2]
    wspec = pl.BlockSpec((None, k, tn), lambda n, m: (idx, 0, n))
    return pl.pallas_call(
        _swiglu_kernel,
        grid=(n_cols // tn, m_rows // tm),
        in_specs=[pl.BlockSpec((tm, k), lambda n, m: (m, 0)), wspec, wspec],
        out_specs=pl.BlockSpec((tm, tn), lambda n, m: (m, n)),
        out_shape=jax.ShapeDtypeStruct((m_rows, n_cols), BF16),
        scratch_shapes=[pltpu.VMEM((k, tn), BF16), pltpu.VMEM((k, tn), BF16)],
        compiler_params=_params(("arbitrary", "arbitrary")),
        name="dense_gate_up",
    )(a, wg, wu)


def _t5_bucket_idx(dist):
    n = jnp.maximum(dist, 0)
    max_exact = NUM_BUCKETS // 2
    ratio = jnp.log(jnp.maximum(n, max_exact).astype(F32) / max_exact) / math.log(MAX_DISTANCE / max_exact)
    large = max_exact + (ratio * (NUM_BUCKETS - max_exact)).astype(jnp.int32)
    large = jnp.minimum(large, NUM_BUCKETS - 1)
    return jnp.where(n < max_exact, n, large)


def _bias_kernel(idx_ref, tab_ref, o_ref):
    idx = idx_ref[...]
    acc = jnp.full(idx.shape, NEG_INF, F32)
    for b in range(NUM_BUCKETS):
        acc = jnp.where(idx == b, tab_ref[b:b + 1, :], acc)
    o_ref[...] = acc


def _bias_call(idx, tab_rows):
    groups = tab_rows.shape[0]
    rows, cols = idx.shape
    return pl.pallas_call(
        _bias_kernel,
        grid=(groups,),
        in_specs=[pl.BlockSpec((rows, cols), lambda g: (0, 0)),
                  pl.BlockSpec((None, NUM_BUCKETS, cols), lambda g: (g, 0, 0))],
        out_specs=pl.BlockSpec((None, rows, cols), lambda g: (g, 0, 0)),
        out_shape=jax.ShapeDtypeStruct((groups, rows, cols), F32),
        compiler_params=_params(("arbitrary",)),
        name="rel_bias",
    )(idx, tab_rows)


def _band_bucket_idx(step, max_dist):
    i = lax.broadcasted_iota(jnp.int32, (BLOCK, 2 * BLOCK), 0)
    c = lax.broadcasted_iota(jnp.int32, (BLOCK, 2 * BLOCK), 1)
    dist = (i - c + BLOCK) * step
    valid = (dist >= 0) & (dist <= max_dist)
    return jnp.where(valid, _t5_bucket_idx(dist), -1)


def _dilation_multiplicity(dist):
    mult = jnp.zeros(dist.shape, jnp.int32)
    for w, r in DILATED:
        mult = mult + ((dist >= 0) & (dist <= w) & (dist % r == 0)).astype(jnp.int32)
    return mult


def _lane_half_mask():
    return lax.broadcasted_iota(jnp.int32, (BLOCK, LANES), 1) < HEAD_DIM


def _dup_head(t, kv):
    rolled = pltpu.roll(t, HEAD_DIM, axis=1)
    low = lax.broadcasted_iota(jnp.int32, t.shape, 1) < HEAD_DIM
    return jnp.where(low, t, rolled) if kv == 0 else jnp.where(low, rolled, t)


def _nt_dot(a, b):
    return lax.dot_general(a, b, (((1,), (1,)), ((), ())), preferred_element_type=F32)


def _band_kernel(*refs, gqa):
    if gqa:
        sink_ref, q_ref, kvp_ref, kvc_ref, bm_ref, o_ref = refs
    else:
        q_ref, kp_ref, kc_ref, vp_ref, vc_ref, bm_ref, acc_ref, st_ref = refs
    first = pl.program_id(2) == 0
    low = _lane_half_mask()
    if not gqa:
        st_ref[...] = jnp.zeros(st_ref.shape, F32)
    if gqa:
        kprev = [_dup_head(kvp_ref[:, 0:LANES], kv).astype(BF16) for kv in range(N_KV_A)]
        kcur = [_dup_head(kvc_ref[:, 0:LANES], kv).astype(BF16) for kv in range(N_KV_A)]
        vprev = [_dup_head(kvp_ref[:, LANES:2 * LANES], kv).astype(BF16) for kv in range(N_KV_A)]
        vcur = [_dup_head(kvc_ref[:, LANES:2 * LANES], kv).astype(BF16) for kv in range(N_KV_A)]
    for p in range(q_ref.shape[1] // LANES):
        cols = slice(p * LANES, (p + 1) * LANES)
        q2 = q_ref[:, cols] * ATTN_SCALE
        if gqa:
            kv = (2 * p) // (N_HEADS_A // N_KV_A)
            kp, kc, vp, vc = kprev[kv], kcur[kv], vprev[kv], vcur[kv]
        else:
            kp, kc = kp_ref[:, cols].astype(BF16), kc_ref[:, cols].astype(BF16)
            vp, vc = vp_ref[:, cols].astype(BF16), vc_ref[:, cols].astype(BF16)
        halves = []
        for half in range(2):
            h = 2 * p + half
            qh = jnp.where(low if half == 0 else ~low, q2, 0.0).astype(BF16)
            sp = _nt_dot(qh, kp) + jnp.where(first, NEG_INF, bm_ref[h, :, 0:BLOCK])
            sc = _nt_dot(qh, kc) + bm_ref[h, :, BLOCK:2 * BLOCK]
            m = jnp.maximum(jnp.max(sp, axis=1, keepdims=True), jnp.max(sc, axis=1, keepdims=True))
            if gqa:
                m = jnp.maximum(m, sink_ref[h])
            pp = jnp.exp(sp - m)
            pc = jnp.exp(sc - m)
            l = jnp.sum(pp, axis=1, keepdims=True) + jnp.sum(pc, axis=1, keepdims=True)
            o = (jnp.dot(pp.astype(BF16), vp, preferred_element_type=F32)
                 + jnp.dot(pc.astype(BF16), vc, preferred_element_type=F32))
            if gqa:
                l = l + jnp.exp(sink_ref[h] - m)
                o = o / l
            else:
                st_ref[:, h:h + 1] = m
                st_ref[:, N_HEADS_B + h:N_HEADS_B + h + 1] = l
            halves.append(o)
        out = jnp.where(low, halves[0], halves[1])
        if gqa:
            o_ref[:, cols] = out
        else:
            acc_ref[:, cols] = out


def _swa_prompt_call(qa, kva, bm, sink):
    nb = SEQ // BLOCK
    cur = lambda b, c, n: (b * nb + n, 0)
    prev = lambda b, c, n: (b * nb + jnp.maximum(n - 1, 0), 0)
    return pl.pallas_call(
        functools.partial(_band_kernel, gqa=True),
        grid=(BATCH, 1, nb),
        in_specs=[pl.BlockSpec(memory_space=pltpu.SMEM),
                  pl.BlockSpec((BLOCK, WIDTH_A), cur),
                  pl.BlockSpec((BLOCK, WIDTH_KVA), prev),
                  pl.BlockSpec((BLOCK, WIDTH_KVA), cur),
                  pl.BlockSpec((N_HEADS_A, BLOCK, 2 * BLOCK), lambda b, c, n: (0, 0, 0))],
        out_specs=pl.BlockSpec((BLOCK, WIDTH_A), cur),
        out_shape=jax.ShapeDtypeStruct((N_PROMPT, WIDTH_A), F32),
        compiler_params=_params(("arbitrary", "arbitrary", "arbitrary")),
        name="swa_prompt",
    )(sink, qa, kva, kva, bm)


def _dilated_prompt_call(qb, kb, vb, bm, r):
    nb = SEQ // (BLOCK * r)
    view = lambda t: t.reshape(N_TOK // r, r * WIDTH_B)
    cur = lambda b, c, n: (b * nb + n, c)
    prev = lambda b, c, n: (b * nb + jnp.maximum(n - 1, 0), c)
    acc, st = pl.pallas_call(
        functools.partial(_band_kernel, gqa=False),
        grid=(BATCH, r, nb),
        in_specs=[pl.BlockSpec((BLOCK, WIDTH_B), cur),
                  pl.BlockSpec((BLOCK, WIDTH_B), prev),
                  pl.BlockSpec((BLOCK, WIDTH_B), cur),
                  pl.BlockSpec((BLOCK, WIDTH_B), prev),
                  pl.BlockSpec((BLOCK, WIDTH_B), cur),
                  pl.BlockSpec((N_HEADS_B, BLOCK, 2 * BLOCK), lambda b, c, n: (0, 0, 0))],
        out_specs=[pl.BlockSpec((BLOCK, WIDTH_B), cur),
                   pl.BlockSpec((BLOCK, LANES), cur)],
        out_shape=[jax.ShapeDtypeStruct((N_PROMPT // r, r * WIDTH_B), F32),
                   jax.ShapeDtypeStruct((N_PROMPT // r, r * LANES), F32)],
        compiler_params=_params(("arbitrary", "arbitrary", "arbitrary")),
        name=f"dilated_prompt_r{r}",
    )(view(qb), view(kb), view(kb), view(vb), view(vb), bm)
    return acc.reshape(N_PROMPT, WIDTH_B), st.reshape(N_PROMPT, LANES)


def _merge_kernel(oa_ref, a1_ref, a2_ref, a3_ref, s1_ref, s2_ref, s3_ref, ga_ref, gb_ref, o_ref, ob_ref):
    accs = (a1_ref, a2_ref, a3_ref)
    stats = (s1_ref, s2_ref, s3_ref)
    tm = oa_ref.shape[0]
    low = lax.broadcasted_iota(jnp.int32, (tm, LANES), 1) < HEAD_DIM
    for p in range(WIDTH_B // LANES):
        cols = slice(p * LANES, (p + 1) * LANES)

        def pair(ref, base):
            return jnp.where(low, ref[:, base + 2 * p:base + 2 * p + 1], ref[:, base + 2 * p + 1:base + 2 * p + 2])

        ms = [pair(s, 0) for s in stats]
        ls = [pair(s, N_HEADS_B) for s in stats]
        top = jnp.maximum(jnp.maximum(ms[0], ms[1]), ms[2])
        ws = [jnp.exp(m - top) for m in ms]
        den = ws[0] * ls[0] + ws[1] * ls[1] + ws[2] * ls[2]
        num = ws[0] * accs[0][:, cols] + ws[1] * accs[1][:, cols] + ws[2] * accs[2][:, cols]
        ob_ref[:, cols] = num / den
    o_ref[:, 0:WIDTH_A] = _rmsnorm_rows(oa_ref[...], ga_ref[...]).astype(o_ref.dtype)
    o_ref[:, WIDTH_A:] = _rmsnorm_rows(ob_ref[...], gb_ref[...]).astype(o_ref.dtype)


def _merge_call(oa, accs, stats, ga, gb, tm=256):
    row = lambda w: pl.BlockSpec((tm, w), lambda i: (i, 0))
    vec = lambda w: pl.BlockSpec((1, w), lambda i: (0, 0))
    return pl.pallas_call(
        _merge_kernel,
        grid=(N_PROMPT // tm,),
        in_specs=[row(WIDTH_A)] + [row(WIDTH_B)] * 3 + [row(LANES)] * 3 + [vec(WIDTH_A), vec(WIDTH_B)],
        out_specs=row(WIDTH_A + WIDTH_B),
        out_shape=jax.ShapeDtypeStruct((N_PROMPT, WIDTH_A + WIDTH_B), BF16),
        scratch_shapes=[pltpu.VMEM((tm, WIDTH_B), F32)],
        compiler_params=_params(("arbitrary",)),
        name="merge_prompt",
    )(oa, *accs, *stats, ga.reshape(1, -1), gb.reshape(1, -1))


NEW_ROWS = 128
SINK_ROW = DEC_SEQ


def _pad_rows(t, rows):
    return jnp.concatenate([t, jnp.zeros((rows - t.shape[0], t.shape[1]), t.dtype)], axis=0)


def _sample_kernel(qa_ref, kva_ref, qb_ref, kb_ref, vb_ref, cak_ref, cav_ref, cbk_ref, cbv_ref,
                   bac_ref, ban_ref, sink_ref, bbc_ref, bbn_ref, mbc_ref, mbn_ref, ga_ref, gb_ref, o_ref):
    group = N_HEADS_A // N_KV_A
    lane8 = lax.broadcasted_iota(jnp.int32, (DEC_SEQ, LANES), 1) < HEAD_DIM

    rows = []
    for p in range(WIDTH_A // LANES):
        blk = qa_ref[:, p * LANES:(p + 1) * LANES] * ATTN_SCALE
        rolled = pltpu.roll(blk, HEAD_DIM, axis=1)
        if (2 * p) // group == 0:
            rows += [jnp.where(lane8, blk, 0.0), jnp.where(lane8, rolled, 0.0)]
        else:
            rows += [jnp.where(lane8, 0.0, rolled), jnp.where(lane8, 0.0, blk)]
    qat = jnp.concatenate(rows, axis=0).astype(BF16)
    kan = _pad_rows(kva_ref[:, 0:LANES], NEW_ROWS)
    van = _pad_rows(kva_ref[:, LANES:2 * LANES], NEW_ROWS)
    row_id = lax.broadcasted_iota(jnp.int32, (NEW_ROWS, LANES), 0)
    sc = _nt_dot(cak_ref[...].astype(BF16), qat) + bac_ref[...]
    sn = _nt_dot(kan.astype(BF16), qat) + jnp.where(row_id == SINK_ROW, sink_ref[...], ban_ref[...])
    m = jnp.maximum(jnp.max(sc, axis=0, keepdims=True), jnp.max(sn, axis=0, keepdims=True))
    pct = jnp.exp(sc - m).T
    pnt = jnp.exp(sn - m).T
    l_col = jnp.sum(pct, axis=1, keepdims=True) + jnp.sum(pnt, axis=1, keepdims=True)
    pct = pct.astype(BF16)
    pnt = pnt.astype(BF16)
    res = []
    for kv in range(N_KV_A):
        vc = _dup_head(cav_ref[...], kv).astype(BF16)
        vn = _dup_head(van, kv).astype(BF16)
        res.append((jnp.dot(pct, vc, preferred_element_type=F32)
                    + jnp.dot(pnt, vn, preferred_element_type=F32)) / l_col)
    pairs = []
    for p in range(WIDTH_A // LANES):
        r = res[(2 * p) // group]
        pairs.append(jnp.where(lane8, r[16 * p:16 * p + 8, :], r[16 * p + 8:16 * p + 16, :]))
    oa = jnp.concatenate(pairs, axis=1)

    q8 = qb_ref[...] * ATTN_SCALE
    qrep = jnp.concatenate([q8] * N_HEADS_B, axis=0)
    r_id = lax.broadcasted_iota(jnp.int32, qrep.shape, 0) // DEC_SEQ
    c_id = lax.broadcasted_iota(jnp.int32, qrep.shape, 1) // HEAD_DIM
    qbt = jnp.where(r_id == c_id, qrep, 0.0).astype(BF16)
    kbn = _pad_rows(kb_ref[...], NEW_ROWS).astype(BF16)
    vbn = _pad_rows(vb_ref[...], NEW_ROWS).astype(BF16)
    sc = _nt_dot(cbk_ref[...].astype(BF16), qbt) + bbc_ref[...]
    sn = _nt_dot(kbn, qbt) + bbn_ref[...]
    m = jnp.maximum(jnp.max(sc, axis=0, keepdims=True), jnp.max(sn, axis=0, keepdims=True))
    pct = (jnp.exp(sc - m) * mbc_ref[...]).T
    pnt = (jnp.exp(sn - m) * mbn_ref[...]).T
    l_col = jnp.sum(pct, axis=1, keepdims=True) + jnp.sum(pnt, axis=1, keepdims=True)
    rb = (jnp.dot(pct.astype(BF16), cbv_ref[...].astype(BF16), preferred_element_type=F32)
          + jnp.dot(pnt.astype(BF16), vbn, preferred_element_type=F32)) / l_col
    pairs = []
    for p in range(WIDTH_B // LANES):
        blk = rb[16 * p:16 * p + 16, p * LANES:(p + 1) * LANES]
        pairs.append(jnp.where(lane8, blk[0:8, :], blk[8:16, :]))
    ob = jnp.concatenate(pairs, axis=1)

    o_ref[:, 0:WIDTH_A] = _rmsnorm_rows(oa, ga_ref[...])
    o_ref[:, WIDTH_A:] = _rmsnorm_rows(ob, gb_ref[...])


def _sample_call(qa, kva, qb, kb, vb, cak, cav, cbk, cbv, consts, sink_row, ga, gb, layer):
    first = N_PROMPT // DEC_SEQ
    new = lambda w: pl.BlockSpec((DEC_SEQ, w), lambda b: (first + b, 0))
    cache = lambda rows, w: pl.BlockSpec((None, None, rows, w), lambda b: (layer, b, 0, 0))
    const = lambda rows, w: pl.BlockSpec((rows, w), lambda b: (0, 0))
    la, lb = cak.shape[2], cbk.shape[2]
    return pl.pallas_call(
        _sample_kernel,
        grid=(DEC_BATCH,),
        in_specs=[new(WIDTH_A), new(WIDTH_KVA), new(WIDTH_B), new(WIDTH_B), new(WIDTH_B),
                  cache(la, LANES), cache(la, LANES), cache(lb, WIDTH_B), cache(lb, WIDTH_B),
                  const(la, LANES), const(NEW_ROWS, LANES), const(1, LANES),
                  const(lb, LANES), const(NEW_ROWS, LANES), const(lb, LANES), const(NEW_ROWS, LANES),
                  const(1, WIDTH_A), const(1, WIDTH_B)],
        out_specs=pl.BlockSpec((DEC_SEQ, WIDTH_A + WIDTH_B), lambda b: (b, 0)),
        out_shape=jax.ShapeDtypeStruct((N_SAMPLE, WIDTH_A + WIDTH_B), F32),
        compiler_params=_params(("arbitrary",)),
        name="sample_attention",
    )(qa, kva, qb, kb, vb, cak, cav, cbk, cbv,
      consts["bac"], consts["ban"], sink_row, consts["bbc"], consts["bbn"], consts["mbc"], consts["mbn"],
      ga.reshape(1, -1), gb.reshape(1, -1))


def _sample_consts(rel_bias):
    la, lb = WINDOW_A, WINDOW_B
    t = lax.broadcasted_iota(jnp.int32, (1, LANES), 1) % DEC_SEQ
    tab_a = jnp.repeat(rel_bias[:, :N_HEADS_A], DEC_SEQ, axis=1)[None]
    tab_b = jnp.repeat(rel_bias[:, N_HEADS_A:], DEC_SEQ, axis=1)[None]

    def idx_of(dist, valid):
        return jnp.where(valid, _t5_bucket_idx(dist), -1)

    i_a = lax.broadcasted_iota(jnp.int32, (la, LANES), 0)
    dist = la + t - i_a
    bac = _bias_call(idx_of(dist, dist <= WINDOW_A - 1), tab_a)[0]
    j = lax.broadcasted_iota(jnp.int32, (NEW_ROWS, LANES), 0)
    dist_n = t - j
    valid_n = (dist_n >= 0) & (j < DEC_SEQ)
    ban = _bias_call(idx_of(dist_n, valid_n), tab_a)[0]
    i_b = lax.broadcasted_iota(jnp.int32, (lb, LANES), 0)
    dist_b = lb + t - i_b
    mult_c = _dilation_multiplicity(dist_b)
    bbc = _bias_call(idx_of(dist_b, mult_c > 0), tab_b)[0]
    mult_n = jnp.where(valid_n, _dilation_multiplicity(dist_n), 0)
    bbn = _bias_call(idx_of(dist_n, mult_n > 0), tab_b)[0]
    return {"bac": bac, "ban": ban, "bbc": bbc, "bbn": bbn,
            "mbc": mult_c.astype(F32), "mbn": mult_n.astype(F32)}


def _router_kernel(x_ref, rt_ref, idx_ref, gate_ref):
    logits = lax.dot_general(rt_ref[...], x_ref[...], (((1,), (1,)), ((), ())),
                             precision=lax.Precision.HIGHEST, preferred_element_type=F32)
    e_id = lax.broadcasted_iota(jnp.int32, logits.shape, 0)
    m1 = jnp.max(logits, axis=0, keepdims=True)
    i1 = jnp.min(jnp.where(logits == m1, e_id, N_EXPERTS), axis=0, keepdims=True)
    rest = jnp.where(e_id == i1, -jnp.inf, logits)
    m2 = jnp.max(rest, axis=0, keepdims=True)
    i2 = jnp.min(jnp.where(rest == m2, e_id, N_EXPERTS), axis=0, keepdims=True)
    e2 = jnp.exp(m2 - m1)
    den = 1.0 + e2
    idx_ref[0:1, :] = i1
    idx_ref[1:2, :] = i2
    gate_ref[0:1, :] = 1.0 / den
    gate_ref[1:2, :] = e2 / den


def _router_call(xn, router_t, tm=TM):
    return pl.pallas_call(
        _router_kernel,
        grid=(N_TOK // tm,),
        in_specs=[pl.BlockSpec((tm, D_MODEL), lambda i: (i, 0)),
                  pl.BlockSpec((N_EXPERTS, D_MODEL), lambda i: (0, 0))],
        out_specs=[pl.BlockSpec((TOP_K, tm), lambda i: (0, i)),
                   pl.BlockSpec((TOP_K, tm), lambda i: (0, i))],
        out_shape=[jax.ShapeDtypeStruct((TOP_K, N_TOK), jnp.int32),
                   jax.ShapeDtypeStruct((TOP_K, N_TOK), F32)],
        compiler_params=_params(("arbitrary",)),
        name="moe_router",
    )(xn, router_t)


def _routing_plan(top_i):
    e = top_i.T.reshape(-1)
    order = jnp.argsort(e, stable=True)
    onehot = (e[:, None] == jnp.arange(N_EXPERTS)[None, :]).astype(jnp.int32)
    counts = jnp.sum(onehot, axis=0)
    starts = jnp.cumsum(counts) - counts
    padded = (counts + MOE_ROWS - 1) // MOE_ROWS * MOE_ROWS
    pstarts = jnp.cumsum(padded) - padded
    blk_start = jnp.arange(N_MOE_BLOCKS) * MOE_ROWS
    blk_e = jnp.minimum(jnp.sum(blk_start[:, None] >= (pstarts + padded)[None, :], axis=1), N_EXPERTS - 1)
    n_used = (jnp.sum(padded) // MOE_ROWS).astype(jnp.int32).reshape(1)
    row = jnp.arange(N_MOE_ROWS)
    row_e = blk_e[row // MOE_ROWS]
    within = row - pstarts[row_e]
    src = order[jnp.clip(starts[row_e] + within, 0, N_ASSIGN - 1)]
    row_tok = jnp.where(within < counts[row_e], src // TOP_K, 0).astype(jnp.int32)
    rank = jnp.take_along_axis(jnp.cumsum(onehot, axis=0) - onehot, e[:, None], axis=1)[:, 0]
    dest = (pstarts[e] + rank).astype(jnp.int32)
    dest_kt = dest.reshape(N_TOK, TOP_K).T.reshape(-1)
    return blk_e.astype(jnp.int32), n_used, row_tok, dest_kt


def _row_copy(src_hbm, row, dst, slot, sem):
    return pltpu.make_async_copy(
        src_hbm.at[pl.ds(pl.multiple_of(row * ROW_CHUNKS, ROW_CHUNKS), ROW_CHUNKS), :],
        dst.at[pl.ds(pl.multiple_of(slot * ROW_CHUNKS, ROW_CHUNKS), ROW_CHUNKS), :],
        sem)


def _wait_rows(src_hbm, dst, sem):
    pltpu.make_async_copy(src_hbm.at[pl.ds(0, dst.shape[0]), :], dst, sem).wait()


def _gather_kernel(tok_ref, x_hbm, o_ref, buf, sem):
    base = pl.program_id(0) * MOE_ROWS

    def issue(i, c):
        _row_copy(x_hbm, tok_ref[base + i], buf, i, sem).start()
        return c

    lax.fori_loop(0, MOE_ROWS, issue, 0)
    _wait_rows(x_hbm, buf, sem)
    for j in range(ROW_CHUNKS):
        o_ref[:, j * LANES:(j + 1) * LANES] = buf[pl.ds(j, MOE_ROWS, stride=ROW_CHUNKS), :].astype(o_ref.dtype)


def _gather_call(row_tok, x_chunks):
    grid_spec = pltpu.PrefetchScalarGridSpec(
        num_scalar_prefetch=1,
        grid=(N_MOE_BLOCKS,),
        in_specs=[pl.BlockSpec(memory_space=pl.ANY)],
        out_specs=pl.BlockSpec((MOE_ROWS, D_MODEL), lambda b, tok: (b, 0)),
        scratch_shapes=[pltpu.VMEM((MOE_ROWS * ROW_CHUNKS, LANES), F32), pltpu.SemaphoreType.DMA(())],
    )
    return pl.pallas_call(
        _gather_kernel,
        grid_spec=grid_spec,
        out_shape=jax.ShapeDtypeStruct((N_MOE_ROWS, D_MODEL), BF16),
        compiler_params=_params(("arbitrary",)),
        name="moe_dispatch",
    )(row_tok, x_chunks)


def _expert_changed(be_ref, nu_ref):
    b = pl.program_id(1)
    last = jnp.minimum(b, nu_ref[0] - 1)
    return (b < nu_ref[0]) & ((b == 0) | (be_ref[last] != be_ref[jnp.maximum(last - 1, 0)]))


def _moe_up_kernel(be_ref, nu_ref, a_ref, wg_ref, wu_ref, o_ref, wgbf_ref, wubf_ref):
    @pl.when(_expert_changed(be_ref, nu_ref))
    def _():
        _cast_weight(wg_ref, wgbf_ref)
        _cast_weight(wu_ref, wubf_ref)

    used = pl.program_id(1) < nu_ref[0]

    @pl.when(used)
    def _():
        a = a_ref[...]
        g = jnp.dot(a, wgbf_ref[...], preferred_element_type=F32)
        u = jnp.dot(a, wubf_ref[...], preferred_element_type=F32)
        o_ref[...] = _silu_mul(g, u).astype(o_ref.dtype)

    @pl.when(jnp.logical_not(used))
    def _():
        o_ref[...] = jnp.zeros(o_ref.shape, o_ref.dtype)


def _moe_down_kernel(be_ref, nu_ref, a_ref, w_ref, o_ref, wbf_ref):
    @pl.when(_expert_changed(be_ref, nu_ref))
    def _():
        _cast_weight(w_ref, wbf_ref)

    used = pl.program_id(1) < nu_ref[0]

    @pl.when(used)
    def _():
        o_ref[...] = jnp.dot(a_ref[...], wbf_ref[...], preferred_element_type=F32)

    @pl.when(jnp.logical_not(used))
    def _():
        o_ref[...] = jnp.zeros(o_ref.shape, o_ref.dtype)


def _moe_matmul_call(kernel, blk_e, n_used, a, weights, idx, tn, out_dtype, name):
    k = a.shape[1]
    n_cols = weights[0].shape[3]
    last = lambda b, nu: jnp.minimum(b, nu[0] - 1)
    wspec = pl.BlockSpec((None, None, k, tn), lambda n, b, be, nu: (idx, be[last(b, nu)], 0, n))
    grid_spec = pltpu.PrefetchScalarGridSpec(
        num_scalar_prefetch=2,
        grid=(n_cols // tn, N_MOE_BLOCKS),
        in_specs=[pl.BlockSpec((MOE_ROWS, k), lambda n, b, be, nu: (last(b, nu), 0))] + [wspec] * len(weights),
        out_specs=pl.BlockSpec((MOE_ROWS, tn), lambda n, b, be, nu: (b, n)),
        scratch_shapes=[pltpu.VMEM((k, tn), BF16)] * len(weights),
    )
    return pl.pallas_call(
        kernel,
        grid_spec=grid_spec,
        out_shape=jax.ShapeDtypeStruct((N_MOE_ROWS, n_cols), out_dtype),
        compiler_params=_params(("arbitrary", "arbitrary")),
        name=name,
    )(blk_e, n_used, a, *weights)


COMBINE_ROWS = 256


def _combine_kernel(dest_ref, x_ref, g_ref, y_hbm, o_ref, buf0, buf1, sems):
    base = pl.program_id(0) * COMBINE_ROWS
    bufs = (buf0, buf1)

    def issue(i, c):
        for k in range(TOP_K):
            _row_copy(y_hbm, dest_ref[k * N_TOK + base + i], bufs[k], i, sems.at[k]).start()
        return c

    lax.fori_loop(0, COMBINE_ROWS, issue, 0)
    for k in range(TOP_K):
        _wait_rows(y_hbm, bufs[k], sems.at[k])
    g0 = g_ref[:, 0:1]
    g1 = g_ref[:, 1:2]
    for j in range(ROW_CHUNKS):
        cols = slice(j * LANES, (j + 1) * LANES)
        rows = pl.ds(j, COMBINE_ROWS, stride=ROW_CHUNKS)
        o_ref[:, cols] = x_ref[:, cols] + (g0 * buf0[rows, :] + g1 * buf1[rows, :])


def _combine_call(dest_kt, x, gates, y_chunks):
    grid_spec = pltpu.PrefetchScalarGridSpec(
        num_scalar_prefetch=1,
        grid=(N_TOK // COMBINE_ROWS,),
        in_specs=[pl.BlockSpec((COMBINE_ROWS, D_MODEL), lambda i, d: (i, 0)),
                  pl.BlockSpec((COMBINE_ROWS, TOP_K), lambda i, d: (i, 0)),
                  pl.BlockSpec(memory_space=pl.ANY)],
        out_specs=pl.BlockSpec((COMBINE_ROWS, D_MODEL), lambda i, d: (i, 0)),
        scratch_shapes=[pltpu.VMEM((COMBINE_ROWS * ROW_CHUNKS, LANES), F32),
                        pltpu.VMEM((COMBINE_ROWS * ROW_CHUNKS, LANES), F32),
                        pltpu.SemaphoreType.DMA((TOP_K,))],
    )
    return pl.pallas_call(
        _combine_kernel,
        grid_spec=grid_spec,
        out_shape=jax.ShapeDtypeStruct((N_TOK, D_MODEL), F32),
        compiler_params=_params(("arbitrary",)),
        name="moe_combine",
    )(dest_kt, x, gates, y_chunks)


def _moe_layer(x, g, router, wg, wu, wd, idx):
    xn = _norm_call(x, g, F32)
    top_i, gates = _router_call(xn, router[idx].T)
    blk_e, n_used, row_tok, dest_kt = _routing_plan(top_i)
    xs = _gather_call(row_tok, xn.reshape(N_TOK * ROW_CHUNKS, LANES))
    h = _moe_matmul_call(_moe_up_kernel, blk_e, n_used, xs, (wg, wu), idx, 1024, BF16, "moe_gate_up")
    y = _moe_matmul_call(_moe_down_kernel, blk_e, n_used, h, (wd,), idx, 512, F32, "moe_down")
    return _combine_call(dest_kt, x, gates.T, y.reshape(N_MOE_ROWS * ROW_CHUNKS, LANES))


def kernel(x_prompt, x_sample, cache_a_k, cache_a_v, cache_b_k, cache_b_v, g_mix_in, w_in, sinks, rel_bias,
           g_out_a, g_out_b, w_out, g_ffn, w_gate_d, w_up_d, w_down_d, router, w_gate_e, w_up_e, w_down_e,
           g_final):
    la, lb = cache_a_k.shape[2], cache_b_k.shape[2]
    x = jnp.concatenate([x_prompt.reshape(N_PROMPT, D_MODEL), x_sample.reshape(N_SAMPLE, D_MODEL)], axis=0)
    cak = cache_a_k.reshape(DEPTH, DEC_BATCH, la, N_KV_A * HEAD_DIM)
    cav = cache_a_v.reshape(DEPTH, DEC_BATCH, la, N_KV_A * HEAD_DIM)
    cbk = cache_b_k.reshape(DEPTH, DEC_BATCH, lb, WIDTH_B)
    cbv = cache_b_v.reshape(DEPTH, DEC_BATCH, lb, WIDTH_B)

    head_rows = lambda tab: jnp.broadcast_to(tab.T[:, :, None], (tab.shape[1], NUM_BUCKETS, 2 * BLOCK))
    tab_a, tab_b = rel_bias[:, :N_HEADS_A], rel_bias[:, N_HEADS_A:]
    bm_a = _bias_call(_band_bucket_idx(1, WINDOW_A - 1), head_rows(tab_a))
    bm_b = [_bias_call(_band_bucket_idx(r, w), head_rows(tab_b)) for w, r in DILATED]
    consts = _sample_consts(rel_bias)

    new_kva, new_kb, new_vb = [], [], []
    for l in range(DEPTH):
        qa, kva, qb, kb, vb = _qkv_call(x, g_mix_in[l], w_in, l)
        oa = _swa_prompt_call(qa, kva, bm_a, sinks[l])
        accs, stats = zip(*[_dilated_prompt_call(qb, kb, vb, bm, r) for bm, (_, r) in zip(bm_b, DILATED)])
        y_prompt = _merge_call(oa, accs, stats, g_out_a[l], g_out_b[l])
        sink_row = jnp.repeat(sinks[l], DEC_SEQ).reshape(1, LANES)
        y_sample = _sample_call(qa, kva, qb, kb, vb, cak, cav, cbk, cbv, consts, sink_row,
                                g_out_a[l], g_out_b[l], l)
        y = jnp.concatenate([y_prompt, y_sample.astype(BF16)], axis=0)
        x = _mm_res_call(y, w_out, l, x, TM, 512, "out_proj")
        if l % 2 == 0:
            hn = _norm_call(x, g_ffn[l], BF16)
            h = _swiglu_call(hn, w_gate_d, w_up_d, l // 2)
            x = _mm_res_call(h, w_down_d, l // 2, x, 384, 512, "dense_down")
        else:
            x = _moe_layer(x, g_ffn[l], router, w_gate_e, w_up_e, w_down_e, l // 2)
        new_kva.append(kva)
        new_kb.append(kb)
        new_vb.append(vb)

    y = _norm_call(x, g_final, F32)
    y_prompt = y[:N_PROMPT].reshape(BATCH, SEQ, D_MODEL)
    y_sample = y[N_PROMPT:].reshape(DEC_BATCH, DEC_SEQ, D_MODEL)

    kva_all = jnp.stack(new_kva)
    kb_all = jnp.stack(new_kb)
    vb_all = jnp.stack(new_vb)
    ka_p = kva_all[:, :N_PROMPT, :LANES].reshape(DEPTH, BATCH, SEQ, N_KV_A, HEAD_DIM)[:, :, -WINDOW_A:]
    va_p = kva_all[:, :N_PROMPT, LANES:].reshape(DEPTH, BATCH, SEQ, N_KV_A, HEAD_DIM)[:, :, -WINDOW_A:]
    kb_p = kb_all[:, :N_PROMPT].reshape(DEPTH, BATCH, SEQ, N_HEADS_B, HEAD_DIM)[:, :, -WINDOW_B:]
    vb_p = vb_all[:, :N_PROMPT].reshape(DEPTH, BATCH, SEQ, N_HEADS_B, HEAD_DIM)[:, :, -WINDOW_B:]
    ka_s = kva_all[:, N_PROMPT:, :LANES].reshape(DEPTH, DEC_BATCH, DEC_SEQ, N_KV_A, HEAD_DIM)
    va_s = kva_all[:, N_PROMPT:, LANES:].reshape(DEPTH, DEC_BATCH, DEC_SEQ, N_KV_A, HEAD_DIM)
    kb_s = kb_all[:, N_PROMPT:].reshape(DEPTH, DEC_BATCH, DEC_SEQ, N_HEADS_B, HEAD_DIM)
    vb_s = vb_all[:, N_PROMPT:].reshape(DEPTH, DEC_BATCH, DEC_SEQ, N_HEADS_B, HEAD_DIM)
    keep_a = min(WINDOW_A, la + DEC_SEQ)
    keep_b = min(WINDOW_B, lb + DEC_SEQ)
    new_ak = jnp.concatenate([cache_a_k, ka_s], axis=2)[:, :, -keep_a:]
    new_av = jnp.concatenate([cache_a_v, va_s], axis=2)[:, :, -keep_a:]
    new_bk = jnp.concatenate([cache_b_k, kb_s], axis=2)[:, :, -keep_b:]
    new_bv = jnp.concatenate([cache_b_v, vb_s], axis=2)[:, :, -keep_b:]
    return (y_prompt, y_sample, ka_p, va_p, kb_p, vb_p, new_ak, new_av, new_bk, new_bv)
```

```python
import functools
import math

import jax
import jax.numpy as jnp
from jax import lax
from jax.experimental import pallas as pl
from jax.experimental.pallas import tpu as pltpu

F32 = jnp.float32
BF16 = jnp.bfloat16

D_MODEL = 2048
BATCH = 2
SEQ = 4096
DEPTH = 4
DEC_BATCH = 32
DEC_SEQ = 8
PAST_LEN = 16384
HEAD_DIM = 64
N_HEADS_A = 16
N_KV_A = 2
N_HEADS_B = 16
WINDOW_A = 128
DILATED = ((128, 1), (512, 4), (2048, 16))
WINDOW_B = 2048
BLOCK = 128
NUM_BUCKETS = 32
MAX_DISTANCE = 2048
ATTN_SCALE = HEAD_DIM ** -0.5
NEG_INF = -1e30
D_FF = 5632
N_EXPERTS = 8
TOP_K = 2
D_FF_EXPERT = 7168
RMS_EPS = 1e-5

N_PROMPT = BATCH * SEQ
N_SAMPLE = DEC_BATCH * DEC_SEQ
N_TOK = N_PROMPT + N_SAMPLE
WIDTH_A = N_HEADS_A * HEAD_DIM
WIDTH_KVA = 2 * N_KV_A * HEAD_DIM
WIDTH_B = N_HEADS_B * HEAD_DIM
LANES = 128
ROW_CHUNKS = D_MODEL // LANES
MOE_ROWS = 256
N_ASSIGN = TOP_K * N_TOK
N_MOE_BLOCKS = N_ASSIGN // MOE_ROWS + N_EXPERTS
N_MOE_ROWS = N_MOE_BLOCKS * MOE_ROWS
VMEM_LIMIT = 56 * 1024 * 1024

TM = 768
TN_QKV = 256
CAST_ROWS = 256


def _params(sem):
    return pltpu.CompilerParams(dimension_semantics=sem, vmem_limit_bytes=VMEM_LIMIT)


def _cast_weight(w_ref, wbf_ref):
    steps = w_ref.shape[0] // CAST_ROWS

    def body(i, c):
        rows = pl.ds(pl.multiple_of(i * CAST_ROWS, CAST_ROWS), CAST_ROWS)
        wbf_ref[rows, :] = w_ref[rows, :].astype(BF16)
        return c

    lax.fori_loop(0, steps, body, 0)


def _rmsnorm_rows(x, g):
    ms = jnp.mean(x * x, axis=-1, keepdims=True)
    return x * lax.rsqrt(ms + RMS_EPS) * g


def _norm_kernel(x_ref, g_ref, o_ref):
    o_ref[...] = _rmsnorm_rows(x_ref[...], g_ref[...]).astype(o_ref.dtype)


def _norm_call(x, g, out_dtype, tm=256):
    rows, width = x.shape
    return pl.pallas_call(
        _norm_kernel,
        grid=(rows // tm,),
        in_specs=[pl.BlockSpec((tm, width), lambda i: (i, 0)),
                  pl.BlockSpec((1, width), lambda i: (0, 0))],
        out_specs=pl.BlockSpec((tm, width), lambda i: (i, 0)),
        out_shape=jax.ShapeDtypeStruct((rows, width), out_dtype),
        compiler_params=_params(("arbitrary",)),
        name="rmsnorm",
    )(x, g.reshape(1, width))


_QKV_TILES = (N_HEADS_A * HEAD_DIM + 2 * N_KV_A * HEAD_DIM + 3 * N_HEADS_B * HEAD_DIM) // TN_QKV
_QKV_SPLIT = ((0, 4), (4, 1), (5, 4), (9, 4), (13, 4))


def _qkv_kernel(x_ref, g_ref, w_ref, qa_ref, kva_ref, qb_ref, kb_ref, vb_ref, xn_ref):
    n = pl.program_id(1)

    @pl.when(n == 0)
    def _():
        xn_ref[...] = _rmsnorm_rows(x_ref[...], g_ref[...]).astype(BF16)

    y = jnp.dot(xn_ref[...], w_ref[...], preferred_element_type=F32)
    for (first, count), o_ref in zip(_QKV_SPLIT, (qa_ref, kva_ref, qb_ref, kb_ref, vb_ref)):
        @pl.when((n >= first) & (n < first + count))
        def _(o_ref=o_ref):
            o_ref[...] = y


def _to_bf16_kernel(w_ref, o_ref):
    o_ref[...] = w_ref[...].astype(BF16)


def _to_bf16_call(w, layer, rows=256):
    k, n = w.shape[1:]
    return pl.pallas_call(
        _to_bf16_kernel,
        grid=(k // rows,),
        in_specs=[pl.BlockSpec((None, rows, n), lambda i: (layer, i, 0))],
        out_specs=pl.BlockSpec((rows, n), lambda i: (i, 0)),
        out_shape=jax.ShapeDtypeStruct((k, n), BF16),
        compiler_params=_params(("arbitrary",)),
        name="weight_to_bf16",
    )(w)


def _qkv_call(x, g, w_in, layer):
    w_bf = _to_bf16_call(w_in, layer)

    def out_spec(first, count):
        return pl.BlockSpec((TM, TN_QKV), lambda m, n: (m, jnp.clip(n - first, 0, count - 1)))

    return pl.pallas_call(
        _qkv_kernel,
        grid=(N_TOK // TM, _QKV_TILES),
        in_specs=[pl.BlockSpec((TM, D_MODEL), lambda m, n: (m, 0)),
                  pl.BlockSpec((1, D_MODEL), lambda m, n: (0, 0)),
                  pl.BlockSpec((D_MODEL, TN_QKV), lambda m, n: (0, n))],
        out_specs=[out_spec(f, c) for f, c in _QKV_SPLIT],
        out_shape=[jax.ShapeDtypeStruct((N_TOK, c * TN_QKV), F32) for _, c in _QKV_SPLIT],
        scratch_shapes=[pltpu.VMEM((TM, D_MODEL), BF16)],
        compiler_params=_params(("arbitrary", "arbitrary")),
        name="qkv_proj",
    )(x, g.reshape(1, D_MODEL), w_bf)


def _mm_res_kernel(a_ref, w_ref, r_ref, o_ref, wbf_ref):
    @pl.when(pl.program_id(1) == 0)
    def _():
        _cast_weight(w_ref, wbf_ref)

    o_ref[...] = r_ref[...] + jnp.dot(a_ref[...], wbf_ref[...], preferred_element_type=F32)


def _mm_res_call(a, w, layer, res, tm, tn, name):
    m_rows, k = a.shape
    n_cols = w.shape[2]
    return pl.pallas_call(
        _mm_res_kernel,
        grid=(n_cols // tn, m_rows // tm),
        in_specs=[pl.BlockSpec((tm, k), lambda n, m: (m, 0)),
                  pl.BlockSpec((None, k, tn), lambda n, m: (layer, 0, n)),
                  pl.BlockSpec((tm, tn), lambda n, m: (m, n))],
        out_specs=pl.BlockSpec((tm, tn), lambda n, m: (m, n)),
        out_shape=jax.ShapeDtypeStruct((m_rows, n_cols), F32),
        scratch_shapes=[pltpu.VMEM((k, tn), BF16)],
        compiler_params=_params(("arbitrary", "arbitrary")),
        name=name,
    )(a, w, res)


def _silu_mul(g, u):
    return g * (1.0 / (1.0 + jnp.exp(-g))) * u


def _swiglu_kernel(a_ref, wg_ref, wu_ref, o_ref, wgbf_ref, wubf_ref):
    @pl.when(pl.program_id(1) == 0)
    def _():
        _cast_weight(wg_ref, wgbf_ref)
        _cast_weight(wu_ref, wubf_ref)

    a = a_ref[...]
    g = jnp.dot(a, wgbf_ref[...], preferred_element_type=F32)
    u = jnp.dot(a, wubf_ref[...], preferred_element_type=F32)
    o_ref[...] = _silu_mul(g, u).astype(o_ref.dtype)


def _swiglu_call(a, wg, wu, idx, tm=TM, tn=512):
    m_rows, k = a.shape
    n_cols = wg.shape[2]
    wspec = pl.BlockSpec((None, k, tn), lambda n, m: (idx, 0, n))
    return pl.pallas_call(
        _swiglu_kernel,
        grid=(n_cols // tn, m_rows // tm),
        in_specs=[pl.BlockSpec((tm, k), lambda n, m: (m, 0)), wspec, wspec],
        out_specs=pl.BlockSpec((tm, tn), lambda n, m: (m, n)),
        out_shape=jax.ShapeDtypeStruct((m_rows, n_cols), BF16),
        scratch_shapes=[pltpu.VMEM((k, tn), BF16), pltpu.VMEM((k, tn), BF16)],
        compiler_params=_params(("arbitrary", "arbitrary")),
        name="dense_gate_up",
    )(a, wg, wu)


def _t5_bucket_idx(dist):
    n = jnp.maximum(dist, 0)
    max_exact = NUM_BUCKETS // 2
    ratio = jnp.log(jnp.maximum(n, max_exact).astype(F32) / max_exact) / math.log(MAX_DISTANCE / max_exact)
    large = max_exact + (ratio * (NUM_BUCKETS - max_exact)).astype(jnp.int32)
    large = jnp.minimum(large, NUM_BUCKETS - 1)
    return jnp.where(n < max_exact, n, large)


def _bias_kernel(idx_ref, tab_ref, o_ref):
    idx = idx_ref[...]
    acc = jnp.full(idx.shape, NEG_INF, F32)
    for b in range(NUM_BUCKETS):
        acc = jnp.where(idx == b, tab_ref[b:b + 1, :], acc)
    o_ref[...] = acc


def _bias_call(idx, tab_rows):
    groups = tab_rows.shape[0]
    rows, cols = idx.shape
    return pl.pallas_call(
        _bias_kernel,
        grid=(groups,),
        in_specs=[pl.BlockSpec((rows, cols), lambda g: (0, 0)),
                  pl.BlockSpec((None, NUM_BUCKETS, cols), lambda g: (g, 0, 0))],
        out_specs=pl.BlockSpec((None, rows, cols), lambda g: (g, 0, 0)),
        out_shape=jax.ShapeDtypeStruct((groups, rows, cols), F32),
        compiler_params=_params(("arbitrary",)),
        name="rel_bias",
    )(idx, tab_rows)


def _band_bucket_idx(step, max_dist):
    i = lax.broadcasted_iota(jnp.int32, (BLOCK, 2 * BLOCK), 0)
    c = lax.broadcasted_iota(jnp.int32, (BLOCK, 2 * BLOCK), 1)
    dist = (i - c + BLOCK) * step
    valid = (dist >= 0) & (dist <= max_dist)
    return jnp.where(valid, _t5_bucket_idx(dist), -1)


def _dilation_multiplicity(dist):
    mult = jnp.zeros(dist.shape, jnp.int32)
    for w, r in DILATED:
        mult = mult + ((dist >= 0) & (dist <= w) & (dist % r == 0)).astype(jnp.int32)
    return mult


def _lane_half_mask():
    return lax.broadcasted_iota(jnp.int32, (BLOCK, LANES), 1) < HEAD_DIM


def _dup_head(t, kv):
    rolled = pltpu.roll(t, HEAD_DIM, axis=1)
    low = lax.broadcasted_iota(jnp.int32, t.shape, 1) < HEAD_DIM
    return jnp.where(low, t, rolled) if kv == 0 else jnp.where(low, rolled, t)


def _nt_dot(a, b):
    return lax.dot_general(a, b, (((1,), (1,)), ((), ())), preferred_element_type=F32)


def _pair_softmax(q2, kp, kc, vp, vc, bias, heads, first, sink_ref=None):
    low = _lane_half_mask()
    ms, ls, os_ = [], [], []
    for half, h in enumerate(heads):
        qh = jnp.where(low if half == 0 else ~low, q2, 0.0).astype(BF16)
        sp = _nt_dot(qh, kp) + jnp.where(first, NEG_INF, bias[h, :, 0:BLOCK])
        sc = _nt_dot(qh, kc) + bias[h, :, BLOCK:2 * BLOCK]
        m = jnp.max(jnp.maximum(sp, sc), axis=1, keepdims=True)
        if sink_ref is not None:
            m = jnp.maximum(m, sink_ref[h])
        pp = jnp.exp(sp - m)
        pc = jnp.exp(sc - m)
        l = jnp.sum(pp + pc, axis=1, keepdims=True)
        if sink_ref is not None:
            l = l + jnp.exp(sink_ref[h] - m)
        ls.append(l)
        os_.append(jnp.dot(pp.astype(BF16), vp, preferred_element_type=F32)
                   + jnp.dot(pc.astype(BF16), vc, preferred_element_type=F32))
        ms.append(m)
    return (jnp.where(low, ms[0], ms[1]), jnp.where(low, ls[0], ls[1]), jnp.where(low, os_[0], os_[1]))


def _swa_kernel(sink_ref, q_ref, kvp_ref, kvc_ref, bm_ref, o_ref):
    first = pl.program_id(1) == 0
    dup = lambda ref, cols: [_dup_head(ref[:, cols], kv).astype(BF16) for kv in range(N_KV_A)]
    kprev, kcur = dup(kvp_ref, slice(0, LANES)), dup(kvc_ref, slice(0, LANES))
    vprev, vcur = dup(kvp_ref, slice(LANES, 2 * LANES)), dup(kvc_ref, slice(LANES, 2 * LANES))
    for p in range(WIDTH_A // LANES):
        cols = slice(p * LANES, (p + 1) * LANES)
        kv = (2 * p) // (N_HEADS_A // N_KV_A)
        _, l, o = _pair_softmax(q_ref[:, cols] * ATTN_SCALE, kprev[kv], kcur[kv], vprev[kv], vcur[kv],
                                bm_ref, (2 * p, 2 * p + 1), first, sink_ref)
        o_ref[:, cols] = o / l


def _swa_prompt_call(qa, kva, bm, sink):
    nb = SEQ // BLOCK
    cur = lambda b, n: (b * nb + n, 0)
    prev = lambda b, n: (b * nb + jnp.maximum(n - 1, 0), 0)
    return pl.pallas_call(
        _swa_kernel,
        grid=(BATCH, nb),
        in_specs=[pl.BlockSpec(memory_space=pltpu.SMEM),
                  pl.BlockSpec((BLOCK, WIDTH_A), cur),
                  pl.BlockSpec((BLOCK, WIDTH_KVA), prev),
                  pl.BlockSpec((BLOCK, WIDTH_KVA), cur),
                  pl.BlockSpec((N_HEADS_A, BLOCK, 2 * BLOCK), lambda b, n: (0, 0, 0))],
        out_specs=pl.BlockSpec((BLOCK, WIDTH_A), cur),
        out_shape=jax.ShapeDtypeStruct((N_PROMPT, WIDTH_A), F32),
        compiler_params=_params(("arbitrary", "arbitrary")),
        name="swa_prompt",
    )(sink, qa, kva, kva, bm)


SUPER = BLOCK * DILATED[-1][1]
HEADS_PER_STEP = 2
STEP_WIDTH = HEADS_PER_STEP * HEAD_DIM


def _dilated_kernel(q_ref, k_ref, v_ref, bm_ref, o_ref, kwin, vwin, m_s, l_s, acc_s):
    n = pl.program_id(2)

    @pl.when(n == 0)
    def _():
        kwin[0:SUPER, :] = jnp.zeros((SUPER, STEP_WIDTH), F32)
        vwin[0:SUPER, :] = jnp.zeros((SUPER, STEP_WIDTH), F32)

    @pl.when(n > 0)
    def _():
        kwin[0:SUPER, :] = kwin[SUPER:2 * SUPER, :]
        vwin[0:SUPER, :] = vwin[SUPER:2 * SUPER, :]

    kwin[SUPER:2 * SUPER, :] = k_ref[...]
    vwin[SUPER:2 * SUPER, :] = v_ref[...]

    def block(d, r, start, first, last):
        rows = pl.ds(start, BLOCK, stride=r) if r > 1 else pl.ds(start, BLOCK)
        krows = lambda off: (pl.ds(SUPER + start + off, BLOCK, stride=r) if r > 1
                             else pl.ds(SUPER + start + off, BLOCK))
        for p in range(STEP_WIDTH // LANES):
            cols = slice(p * LANES, (p + 1) * LANES)
            q2 = q_ref[rows, cols] * ATTN_SCALE
            kp = kwin[krows(-BLOCK * r), cols].astype(BF16)
            kc = kwin[krows(0), cols].astype(BF16)
            vp = vwin[krows(-BLOCK * r), cols].astype(BF16)
            vc = vwin[krows(0), cols].astype(BF16)
            m, l, o = _pair_softmax(q2, kp, kc, vp, vc, bm_ref.at[d], (2 * p, 2 * p + 1), first)
            if d > 0:
                m_old = m_s[rows, cols]
                m_new = jnp.maximum(m_old, m)
                a_old = jnp.exp(m_old - m_new)
                a_new = jnp.exp(m - m_new)
                l = a_old * l_s[rows, cols] + a_new * l
                o = a_old * acc_s[rows, cols] + a_new * o
                m = m_new
            if last:
                o_ref[rows, cols] = o / l
            else:
                m_s[rows, cols] = m
                l_s[rows, cols] = l
                acc_s[rows, cols] = o

    n_dil = len(DILATED)
    for d, (_, r) in enumerate(DILATED):
        span = BLOCK * r
        unroll = min(r, 8)

        def sweep(i, c, d=d, r=r, span=span, unroll=unroll):
            groups = r // unroll
            base = pl.multiple_of((i // groups) * span + (i % groups) * unroll, 8)
            first = (n == 0) & (i // groups == 0)
            for rho in range(unroll):
                block(d, r, base + rho, first, d == n_dil - 1)
            return c

        lax.fori_loop(0, (SUPER // span) * (r // unroll), sweep, 0)


def _dilated_prompt_call(qb, kb, vb, bm):
    tiles = SEQ // SUPER
    tile = pl.BlockSpec((SUPER, STEP_WIDTH), lambda b, g, n: (b * tiles + n, g))
    return pl.pallas_call(
        _dilated_kernel,
        grid=(BATCH, N_HEADS_B // HEADS_PER_STEP, tiles),
        in_specs=[tile, tile, tile,
                  pl.BlockSpec((len(DILATED), HEADS_PER_STEP, BLOCK, 2 * BLOCK), lambda b, g, n: (0, g, 0, 0))],
        out_specs=tile,
        out_shape=jax.ShapeDtypeStruct((N_PROMPT, WIDTH_B), F32),
        scratch_shapes=[pltpu.VMEM((2 * SUPER, STEP_WIDTH), F32), pltpu.VMEM((2 * SUPER, STEP_WIDTH), F32),
                        pltpu.VMEM((SUPER, STEP_WIDTH), F32), pltpu.VMEM((SUPER, STEP_WIDTH), F32),
                        pltpu.VMEM((SUPER, STEP_WIDTH), F32)],
        compiler_params=_params(("arbitrary", "arbitrary", "arbitrary")),
        name="dilated_prompt",
    )(qb, kb, vb, bm)


def _merge_kernel(oa_ref, ob_ref, ga_ref, gb_ref, o_ref):
    o_ref[:, 0:WIDTH_A] = _rmsnorm_rows(oa_ref[...], ga_ref[...]).astype(o_ref.dtype)
    o_ref[:, WIDTH_A:] = _rmsnorm_rows(ob_ref[...], gb_ref[...]).astype(o_ref.dtype)


def _merge_call(oa, ob, ga, gb, tm=256):
    row = lambda w: pl.BlockSpec((tm, w), lambda i: (i, 0))
    vec = lambda w: pl.BlockSpec((1, w), lambda i: (0, 0))
    return pl.pallas_call(
        _merge_kernel,
        grid=(N_PROMPT // tm,),
        in_specs=[row(WIDTH_A), row(WIDTH_B), vec(WIDTH_A), vec(WIDTH_B)],
        out_specs=row(WIDTH_A + WIDTH_B),
        out_shape=jax.ShapeDtypeStruct((N_PROMPT, WIDTH_A + WIDTH_B), BF16),
        compiler_params=_params(("arbitrary",)),
        name="merge_prompt",
    )(oa, ob, ga.reshape(1, -1), gb.reshape(1, -1))


NEW_ROWS = 128
SINK_ROW = DEC_SEQ
HEAD_GROUPS = 2
GROUP_WIDTH = WIDTH_B // HEAD_GROUPS


def _pad_rows(t, rows):
    return jnp.concatenate([t, jnp.zeros((rows - t.shape[0], t.shape[1]), t.dtype)], axis=0)


def _sample_window_a(qa_ref, kva_ref, cak_ref, cav_ref, bac_ref, ban_ref, sink_ref):
    group = N_HEADS_A // N_KV_A
    lane8 = lax.broadcasted_iota(jnp.int32, (DEC_SEQ, LANES), 1) < HEAD_DIM
    rows = []
    for p in range(WIDTH_A // LANES):
        blk = qa_ref[:, p * LANES:(p + 1) * LANES] * ATTN_SCALE
        rolled = pltpu.roll(blk, HEAD_DIM, axis=1)
        if (2 * p) // group == 0:
            rows += [jnp.where(lane8, blk, 0.0), jnp.where(lane8, rolled, 0.0)]
        else:
            rows += [jnp.where(lane8, 0.0, rolled), jnp.where(lane8, 0.0, blk)]
    qat = jnp.concatenate(rows, axis=0).astype(BF16)
    kan = _pad_rows(kva_ref[:, 0:LANES], NEW_ROWS)
    van = _pad_rows(kva_ref[:, LANES:2 * LANES], NEW_ROWS)
    row_id = lax.broadcasted_iota(jnp.int32, (NEW_ROWS, LANES), 0)
    sc = _nt_dot(cak_ref[...].astype(BF16), qat) + bac_ref[...]
    sn = _nt_dot(kan.astype(BF16), qat) + jnp.where(row_id == SINK_ROW, sink_ref[...], ban_ref[...])
    m = jnp.maximum(jnp.max(sc, axis=0, keepdims=True), jnp.max(sn, axis=0, keepdims=True))
    pct = jnp.exp(sc - m).T
    pnt = jnp.exp(sn - m).T
    l_col = jnp.sum(pct, axis=1, keepdims=True) + jnp.sum(pnt, axis=1, keepdims=True)
    pct = pct.astype(BF16)
    pnt = pnt.astype(BF16)
    res = []
    for kv in range(N_KV_A):
        vc = _dup_head(cav_ref[...], kv).astype(BF16)
        vn = _dup_head(van, kv).astype(BF16)
        res.append((jnp.dot(pct, vc, preferred_element_type=F32)
                    + jnp.dot(pnt, vn, preferred_element_type=F32)) / l_col)
    pairs = []
    for p in range(WIDTH_A // LANES):
        r = res[(2 * p) // group]
        pairs.append(jnp.where(lane8, r[16 * p:16 * p + 8, :], r[16 * p + 8:16 * p + 16, :]))
    return jnp.concatenate(pairs, axis=1)


def _shift_in_new(old_ref, new_ref, out_ref):
    feats, window = old_ref.shape
    keep = LANES - DEC_SEQ
    lane = lax.broadcasted_iota(jnp.int32, (feats, LANES), 1)
    tails = []
    for c in range(feats // LANES):
        sq = _pad_rows(new_ref[:, c * LANES:(c + 1) * LANES], LANES).T
        tails.append(pltpu.roll(sq, keep, axis=1))
    nxt = jnp.concatenate(tails, axis=0)
    for j in reversed(range(window // LANES)):
        cur = pltpu.roll(old_ref[:, j * LANES:(j + 1) * LANES], keep, axis=1)
        out_ref[:, j * LANES:(j + 1) * LANES] = jnp.where(lane < keep, cur, nxt)
        nxt = cur


def _sample_kernel(qa_ref, kva_ref, qb_ref, kb_ref, vb_ref, cak_ref, cav_ref, cbk_ref, cbv_ref,
                   bac_ref, ban_ref, sink_ref, bbc_ref, bbn_ref, mbc_ref, mbn_ref, ga_ref, gb_ref,
                   o_ref, oa_s, ob_s):
    g = pl.program_id(1)

    @pl.when(g == 0)
    def _():
        oa_s[...] = _sample_window_a(qa_ref, kva_ref, cak_ref, cav_ref, bac_ref, ban_ref, sink_ref)

    low = lax.broadcasted_iota(jnp.int32, (DEC_SEQ, LANES), 1) < HEAD_DIM
    pairs = []
    for p in range(GROUP_WIDTH // LANES):
        cols = slice(p * LANES, (p + 1) * LANES)
        q2 = qb_ref[:, cols] * ATTN_SCALE
        kt = cbk_ref[cols, :].astype(BF16)
        vt = cbv_ref[cols, :].astype(BF16)
        kn = _pad_rows(kb_ref[:, cols], NEW_ROWS).astype(BF16)
        vn = _pad_rows(vb_ref[:, cols], NEW_ROWS).astype(BF16)
        halves = []
        for half in range(2):
            i = 2 * p + half
            qh = jnp.where(low if half == 0 else ~low, q2, 0.0).astype(BF16)
            sc = jnp.dot(qh, kt, preferred_element_type=F32) + bbc_ref[i]
            sn = _nt_dot(qh, kn) + bbn_ref[i]
            m = jnp.maximum(jnp.max(sc, axis=1, keepdims=True), jnp.max(sn, axis=1, keepdims=True))
            pc = jnp.exp(sc - m) * mbc_ref[...]
            pn = jnp.exp(sn - m) * mbn_ref[...]
            l = jnp.sum(pc, axis=1, keepdims=True) + jnp.sum(pn, axis=1, keepdims=True)
            o = _nt_dot(pc.astype(BF16), vt) + jnp.dot(pn.astype(BF16), vn, preferred_element_type=F32)
            halves.append(o / l)
        pairs.append(jnp.where(low, halves[0], halves[1]))
    ob = jnp.concatenate(pairs, axis=1)
    for k in range(HEAD_GROUPS):
        @pl.when(g == k)
        def _(k=k):
            ob_s[:, k * GROUP_WIDTH:(k + 1) * GROUP_WIDTH] = ob

    @pl.when(g == HEAD_GROUPS - 1)
    def _():
        o_ref[:, 0:WIDTH_A] = _rmsnorm_rows(oa_s[...], ga_ref[...])
        o_ref[:, WIDTH_A:] = _rmsnorm_rows(ob_s[...], gb_ref[...])


def _sample_call(qa, kva, qb, kb, vb, cak, cav, cbk_t, cbv_t, consts, sink_row, ga, gb, layer):
    first = N_PROMPT // DEC_SEQ
    new = lambda w: pl.BlockSpec((DEC_SEQ, w), lambda b, g: (first + b, 0))
    new_g = pl.BlockSpec((DEC_SEQ, GROUP_WIDTH), lambda b, g: (first + b, g))
    la, lb = cak.shape[2], cbk_t.shape[3]
    cache_a = pl.BlockSpec((None, None, la, LANES), lambda b, g: (layer, b, 0, 0))
    cache_b = pl.BlockSpec((None, None, GROUP_WIDTH, lb), lambda b, g: (layer, b, g, 0))
    const = lambda *shape: pl.BlockSpec(shape, lambda b, g: (0,) * len(shape))
    heads = N_HEADS_B // HEAD_GROUPS
    per_head = lambda *shape: pl.BlockSpec((heads,) + shape, lambda b, g: (g,) + (0,) * len(shape))
    in_specs = [new(WIDTH_A), new(WIDTH_KVA), new_g, new_g, new_g, cache_a, cache_a, cache_b, cache_b,
                const(la, LANES), const(NEW_ROWS, LANES), const(1, LANES),
                per_head(DEC_SEQ, lb), per_head(DEC_SEQ, NEW_ROWS), const(DEC_SEQ, lb), const(DEC_SEQ, NEW_ROWS),
                const(1, WIDTH_A), const(1, WIDTH_B)]
    args = [qa, kva, qb, kb, vb, cak, cav, cbk_t, cbv_t,
            consts["bac"], consts["ban"], sink_row, consts["bbc"], consts["bbn"], consts["mbc"], consts["mbn"],
            ga.reshape(1, -1), gb.reshape(1, -1)]
    return pl.pallas_call(
        _sample_kernel,
        grid=(DEC_BATCH, HEAD_GROUPS),
        in_specs=in_specs,
        out_specs=pl.BlockSpec((DEC_SEQ, WIDTH_A + WIDTH_B), lambda b, g: (b, 0)),
        out_shape=jax.ShapeDtypeStruct((N_SAMPLE, WIDTH_A + WIDTH_B), F32),
        scratch_shapes=[pltpu.VMEM((DEC_SEQ, WIDTH_A), F32), pltpu.VMEM((DEC_SEQ, WIDTH_B), F32)],
        compiler_params=_params(("arbitrary", "arbitrary")),
        name="sample_attention",
    )(*args)


def _cache_shift_kernel(old_ref, new_ref, out_ref):
    _shift_in_new(old_ref, new_ref, out_ref)


def _cache_shift_call(cache_t, new_rows):
    lb = cache_t.shape[3]
    block = pl.BlockSpec((None, None, GROUP_WIDTH, lb), lambda l, b, g: (l, b, g, 0))
    return pl.pallas_call(
        _cache_shift_kernel,
        grid=(DEPTH, DEC_BATCH, HEAD_GROUPS),
        in_specs=[block, pl.BlockSpec((None, DEC_SEQ, GROUP_WIDTH), lambda l, b, g: (l, b, g))],
        out_specs=block,
        out_shape=jax.ShapeDtypeStruct(cache_t.shape, F32),
        compiler_params=_params(("arbitrary", "arbitrary", "arbitrary")),
        name="cache_shift",
    )(cache_t, new_rows)


def _sample_consts(rel_bias):
    la, lb = WINDOW_A, WINDOW_B

    def idx_of(dist, valid):
        return jnp.where(valid, _t5_bucket_idx(dist), -1)

    t = lax.broadcasted_iota(jnp.int32, (1, LANES), 1) % DEC_SEQ
    tab_a = jnp.repeat(rel_bias[:, :N_HEADS_A], DEC_SEQ, axis=1)[None]
    i_a = lax.broadcasted_iota(jnp.int32, (la, LANES), 0)
    dist = la + t - i_a
    bac = _bias_call(idx_of(dist, dist <= WINDOW_A - 1), tab_a)[0]
    j = lax.broadcasted_iota(jnp.int32, (NEW_ROWS, LANES), 0)
    dist_n = t - j
    ban = _bias_call(idx_of(dist_n, (dist_n >= 0) & (j < DEC_SEQ)), tab_a)[0]
    tab_b = rel_bias[:, N_HEADS_A:].T[:, :, None]
    tq = lax.broadcasted_iota(jnp.int32, (DEC_SEQ, lb), 0)
    ik = lax.broadcasted_iota(jnp.int32, (DEC_SEQ, lb), 1)
    dist_b = lb + tq - ik
    mult_c = _dilation_multiplicity(dist_b)
    bbc = _bias_call(idx_of(dist_b, mult_c > 0), jnp.broadcast_to(tab_b, (N_HEADS_B, NUM_BUCKETS, lb)))
    tn = lax.broadcasted_iota(jnp.int32, (DEC_SEQ, NEW_ROWS), 0)
    jn = lax.broadcasted_iota(jnp.int32, (DEC_SEQ, NEW_ROWS), 1)
    mult_n = jnp.where(jn < DEC_SEQ, _dilation_multiplicity(tn - jn), 0)
    bbn = _bias_call(idx_of(tn - jn, mult_n > 0), jnp.broadcast_to(tab_b, (N_HEADS_B, NUM_BUCKETS, NEW_ROWS)))
    return {"bac": bac, "ban": ban, "bbc": bbc, "bbn": bbn,
            "mbc": mult_c.astype(F32), "mbn": mult_n.astype(F32)}


def _router_kernel(x_ref, rt_ref, idx_ref, gate_ref):
    logits = lax.dot_general(rt_ref[...], x_ref[...], (((1,), (1,)), ((), ())),
                             precision=lax.Precision.HIGHEST, preferred_element_type=F32)
    e_id = lax.broadcasted_iota(jnp.int32, logits.shape, 0)
    m1 = jnp.max(logits, axis=0, keepdims=True)
    i1 = jnp.min(jnp.where(logits == m1, e_id, N_EXPERTS), axis=0, keepdims=True)
    rest = jnp.where(e_id == i1, -jnp.inf, logits)
    m2 = jnp.max(rest, axis=0, keepdims=True)
    i2 = jnp.min(jnp.where(rest == m2, e_id, N_EXPERTS), axis=0, keepdims=True)
    e2 = jnp.exp(m2 - m1)
    den = 1.0 + e2
    idx_ref[0:1, :] = i1
    idx_ref[1:2, :] = i2
    gate_ref[0:1, :] = 1.0 / den
    gate_ref[1:2, :] = e2 / den


def _router_call(xn, router_t, tm=TM):
    return pl.pallas_call(
        _router_kernel,
        grid=(N_TOK // tm,),
        in_specs=[pl.BlockSpec((tm, D_MODEL), lambda i: (i, 0)),
                  pl.BlockSpec((N_EXPERTS, D_MODEL), lambda i: (0, 0))],
        out_specs=[pl.BlockSpec((TOP_K, tm), lambda i: (0, i)),
                   pl.BlockSpec((TOP_K, tm), lambda i: (0, i))],
        out_shape=[jax.ShapeDtypeStruct((TOP_K, N_TOK), jnp.int32),
                   jax.ShapeDtypeStruct((TOP_K, N_TOK), F32)],
        compiler_params=_params(("arbitrary",)),
        name="moe_router",
    )(xn, router_t)


def _routing_plan(top_i):
    e = top_i.T.reshape(-1)
    order = jnp.argsort(e, stable=True)
    onehot = (e[:, None] == jnp.arange(N_EXPERTS)[None, :]).astype(jnp.int32)
    counts = jnp.sum(onehot, axis=0)
    starts = jnp.cumsum(counts) - counts
    padded = (counts + MOE_ROWS - 1) // MOE_ROWS * MOE_ROWS
    pstarts = jnp.cumsum(padded) - padded
    blk_start = jnp.arange(N_MOE_BLOCKS) * MOE_ROWS
    blk_e = jnp.minimum(jnp.sum(blk_start[:, None] >= (pstarts + padded)[None, :], axis=1), N_EXPERTS - 1)
    n_used = (jnp.sum(padded) // MOE_ROWS).astype(jnp.int32).reshape(1)
    row = jnp.arange(N_MOE_ROWS)
    row_e = blk_e[row // MOE_ROWS]
    within = row - pstarts[row_e]
    src = order[jnp.clip(starts[row_e] + within, 0, N_ASSIGN - 1)]
    row_tok = jnp.where(within < counts[row_e], src // TOP_K, 0).astype(jnp.int32)
    rank = jnp.take_along_axis(jnp.cumsum(onehot, axis=0) - onehot, e[:, None], axis=1)[:, 0]
    dest = (pstarts[e] + rank).astype(jnp.int32)
    dest_kt = dest.reshape(N_TOK, TOP_K).T.reshape(-1)
    return blk_e.astype(jnp.int32), n_used, row_tok, dest_kt


def _row_copy(src_hbm, row, dst, slot, sem):
    return pltpu.make_async_copy(
        src_hbm.at[pl.ds(pl.multiple_of(row * ROW_CHUNKS, ROW_CHUNKS), ROW_CHUNKS), :],
        dst.at[pl.ds(pl.multiple_of(slot * ROW_CHUNKS, ROW_CHUNKS), ROW_CHUNKS), :],
        sem)


def _wait_rows(src_hbm, dst, sem):
    pltpu.make_async_copy(src_hbm.at[pl.ds(0, dst.shape[0]), :], dst, sem).wait()


def _gather_kernel(tok_ref, x_hbm, o_ref, buf, sem):
    base = pl.program_id(0) * MOE_ROWS

    def issue(i, c):
        _row_copy(x_hbm, tok_ref[base + i], buf, i, sem).start()
        return c

    lax.fori_loop(0, MOE_ROWS, issue, 0)
    _wait_rows(x_hbm, buf, sem)
    for j in range(ROW_CHUNKS):
        o_ref[:, j * LANES:(j + 1) * LANES] = buf[pl.ds(j, MOE_ROWS, stride=ROW_CHUNKS), :].astype(o_ref.dtype)


def _gather_call(row_tok, x_chunks):
    grid_spec = pltpu.PrefetchScalarGridSpec(
        num_scalar_prefetch=1,
        grid=(N_MOE_BLOCKS,),
        in_specs=[pl.BlockSpec(memory_space=pl.ANY)],
        out_specs=pl.BlockSpec((MOE_ROWS, D_MODEL), lambda b, tok: (b, 0)),
        scratch_shapes=[pltpu.VMEM((MOE_ROWS * ROW_CHUNKS, LANES), F32), pltpu.SemaphoreType.DMA(())],
    )
    return pl.pallas_call(
        _gather_kernel,
        grid_spec=grid_spec,
        out_shape=jax.ShapeDtypeStruct((N_MOE_ROWS, D_MODEL), BF16),
        compiler_params=_params(("arbitrary",)),
        name="moe_dispatch",
    )(row_tok, x_chunks)


def _expert_changed(be_ref, nu_ref):
    b = pl.program_id(1)
    last = jnp.minimum(b, nu_ref[0] - 1)
    return (b < nu_ref[0]) & ((b == 0) | (be_ref[last] != be_ref[jnp.maximum(last - 1, 0)]))


def _moe_up_kernel(be_ref, nu_ref, a_ref, wg_ref, wu_ref, o_ref, wgbf_ref, wubf_ref):
    @pl.when(_expert_changed(be_ref, nu_ref))
    def _():
        _cast_weight(wg_ref, wgbf_ref)
        _cast_weight(wu_ref, wubf_ref)

    used = pl.program_id(1) < nu_ref[0]

    @pl.when(used)
    def _():
        a = a_ref[...]
        g = jnp.dot(a, wgbf_ref[...], preferred_element_type=F32)
        u = jnp.dot(a, wubf_ref[...], preferred_element_type=F32)
        o_ref[...] = _silu_mul(g, u).astype(o_ref.dtype)

    @pl.when(jnp.logical_not(used))
    def _():
        o_ref[...] = jnp.zeros(o_ref.shape, o_ref.dtype)


def _moe_down_kernel(be_ref, nu_ref, a_ref, w_ref, o_ref, wbf_ref):
    @pl.when(_expert_changed(be_ref, nu_ref))
    def _():
        _cast_weight(w_ref, wbf_ref)

    used = pl.program_id(1) < nu_ref[0]

    @pl.when(used)
    def _():
        o_ref[...] = jnp.dot(a_ref[...], wbf_ref[...], preferred_element_type=F32)

    @pl.when(jnp.logical_not(used))
    def _():
        o_ref[...] = jnp.zeros(o_ref.shape, o_ref.dtype)


def _moe_matmul_call(kernel, blk_e, n_used, a, weights, idx, tn, out_dtype, name):
    k = a.shape[1]
    n_cols = weights[0].shape[3]
    last = lambda b, nu: jnp.minimum(b, nu[0] - 1)
    wspec = pl.BlockSpec((None, None, k, tn), lambda n, b, be, nu: (idx, be[last(b, nu)], 0, n))
    grid_spec = pltpu.PrefetchScalarGridSpec(
        num_scalar_prefetch=2,
        grid=(n_cols // tn, N_MOE_BLOCKS),
        in_specs=[pl.BlockSpec((MOE_ROWS, k), lambda n, b, be, nu: (last(b, nu), 0))] + [wspec] * len(weights),
        out_specs=pl.BlockSpec((MOE_ROWS, tn), lambda n, b, be, nu: (b, n)),
        scratch_shapes=[pltpu.VMEM((k, tn), BF16)] * len(weights),
    )
    return pl.pallas_call(
        kernel,
        grid_spec=grid_spec,
        out_shape=jax.ShapeDtypeStruct((N_MOE_ROWS, n_cols), out_dtype),
        compiler_params=_params(("arbitrary", "arbitrary")),
        name=name,
    )(blk_e, n_used, a, *weights)


COMBINE_ROWS = 256


def _combine_kernel(dest_ref, x_ref, g_ref, y_hbm, o_ref, buf0, buf1, sems):
    base = pl.program_id(0) * COMBINE_ROWS
    bufs = (buf0, buf1)

    def issue(i, c):
        for k in range(TOP_K):
            _row_copy(y_hbm, dest_ref[k * N_TOK + base + i], bufs[k], i, sems.at[k]).start()
        return c

    lax.fori_loop(0, COMBINE_ROWS, issue, 0)
    for k in range(TOP_K):
        _wait_rows(y_hbm, bufs[k], sems.at[k])
    g0 = g_ref[:, 0:1]
    g1 = g_ref[:, 1:2]
    for j in range(ROW_CHUNKS):
        cols = slice(j * LANES, (j + 1) * LANES)
        rows = pl.ds(j, COMBINE_ROWS, stride=ROW_CHUNKS)
        o_ref[:, cols] = x_ref[:, cols] + (g0 * buf0[rows, :] + g1 * buf1[rows, :])


def _combine_call(dest_kt, x, gates, y_chunks):
    grid_spec = pltpu.PrefetchScalarGridSpec(
        num_scalar_prefetch=1,
        grid=(N_TOK // COMBINE_ROWS,),
        in_specs=[pl.BlockSpec((COMBINE_ROWS, D_MODEL), lambda i, d: (i, 0)),
                  pl.BlockSpec((COMBINE_ROWS, TOP_K), lambda i, d: (i, 0)),
                  pl.BlockSpec(memory_space=pl.ANY)],
        out_specs=pl.BlockSpec((COMBINE_ROWS, D_MODEL), lambda i, d: (i, 0)),
        scratch_shapes=[pltpu.VMEM((COMBINE_ROWS * ROW_CHUNKS, LANES), F32),
                        pltpu.VMEM((COMBINE_ROWS * ROW_CHUNKS, LANES), F32),
                        pltpu.SemaphoreType.DMA((TOP_K,))],
    )
    return pl.pallas_call(
        _combine_kernel,
        grid_spec=grid_spec,
        out_shape=jax.ShapeDtypeStruct((N_TOK, D_MODEL), F32),
        compiler_params=_params(("arbitrary",)),
        name="moe_combine",
    )(dest_kt, x, gates, y_chunks)


def _moe_layer(x, g, router, wg, wu, wd, idx):
    xn = _norm_call(x, g, F32)
    top_i, gates = _router_call(xn, router[idx].T)
    blk_e, n_used, row_tok, dest_kt = _routing_plan(top_i)
    xs = _gather_call(row_tok, xn.reshape(N_TOK * ROW_CHUNKS, LANES))
    h = _moe_matmul_call(_moe_up_kernel, blk_e, n_used, xs, (wg, wu), idx, 1024, BF16, "moe_gate_up")
    y = _moe_matmul_call(_moe_down_kernel, blk_e, n_used, h, (wd,), idx, 512, F32, "moe_down")
    return _combine_call(dest_kt, x, gates.T, y.reshape(N_MOE_ROWS * ROW_CHUNKS, LANES))


def kernel(x_prompt, x_sample, cache_a_k, cache_a_v, cache_b_k, cache_b_v, g_mix_in, w_in, sinks, rel_bias,
           g_out_a, g_out_b, w_out, g_ffn, w_gate_d, w_up_d, w_down_d, router, w_gate_e, w_up_e, w_down_e,
           g_final):
    la, lb = cache_a_k.shape[2], cache_b_k.shape[2]
    x = jnp.concatenate([x_prompt.reshape(N_PROMPT, D_MODEL), x_sample.reshape(N_SAMPLE, D_MODEL)], axis=0)
    cak = cache_a_k.reshape(DEPTH, DEC_BATCH, la, N_KV_A * HEAD_DIM)
    cav = cache_a_v.reshape(DEPTH, DEC_BATCH, la, N_KV_A * HEAD_DIM)
    to_t = lambda c: jnp.transpose(c, (0, 1, 3, 4, 2)).reshape(DEPTH, DEC_BATCH, WIDTH_B, lb)
    from_t = lambda c: jnp.transpose(c.reshape(DEPTH, DEC_BATCH, N_HEADS_B, HEAD_DIM, lb), (0, 1, 4, 2, 3))
    cbk_t, cbv_t = to_t(cache_b_k), to_t(cache_b_v)

    head_rows = lambda tab: jnp.broadcast_to(tab.T[:, :, None], (tab.shape[1], NUM_BUCKETS, 2 * BLOCK))
    tab_a, tab_b = rel_bias[:, :N_HEADS_A], rel_bias[:, N_HEADS_A:]
    bm_a = _bias_call(_band_bucket_idx(1, WINDOW_A - 1), head_rows(tab_a))
    bm_b = jnp.stack([_bias_call(_band_bucket_idx(r, w), head_rows(tab_b)) for w, r in DILATED])
    consts = _sample_consts(rel_bias)

    tail = lambda t, keep: jnp.stack([t[(b + 1) * SEQ - keep:(b + 1) * SEQ] for b in range(BATCH)])
    kva_p, kb_p, vb_p, kva_s, kb_s, vb_s = [], [], [], [], [], []
    for l in range(DEPTH):
        qa, kva, qb, kb, vb = _qkv_call(x, g_mix_in[l], w_in, l)
        oa = _swa_prompt_call(qa, kva, bm_a, sinks[l])
        ob = _dilated_prompt_call(qb, kb, vb, bm_b)
        y_prompt = _merge_call(oa, ob, g_out_a[l], g_out_b[l])
        sink_row = jnp.repeat(sinks[l], DEC_SEQ).reshape(1, LANES)
        y_sample = _sample_call(qa, kva, qb, kb, vb, cak, cav, cbk_t, cbv_t, consts, sink_row,
                                g_out_a[l], g_out_b[l], l)
        y = jnp.concatenate([y_prompt, y_sample.astype(BF16)], axis=0)
        x = _mm_res_call(y, w_out, l, x, TM, 512, "out_proj")
        if l % 2 == 0:
            hn = _norm_call(x, g_ffn[l], BF16)
            h = _swiglu_call(hn, w_gate_d, w_up_d, l // 2)
            x = _mm_res_call(h, w_down_d, l // 2, x, 384, 512, "dense_down")
        else:
            x = _moe_layer(x, g_ffn[l], router, w_gate_e, w_up_e, w_down_e, l // 2)
        kva_p.append(tail(kva, min(WINDOW_A, SEQ)))
        kb_p.append(tail(kb, min(WINDOW_B, SEQ)))
        vb_p.append(tail(vb, min(WINDOW_B, SEQ)))
        kva_s.append(kva[N_PROMPT:])
        kb_s.append(kb[N_PROMPT:])
        vb_s.append(vb[N_PROMPT:])

    y = _norm_call(x, g_final, F32)
    y_prompt = y[:N_PROMPT].reshape(BATCH, SEQ, D_MODEL)
    y_sample = y[N_PROMPT:].reshape(DEC_BATCH, DEC_SEQ, D_MODEL)

    kva_p, kva_s = jnp.stack(kva_p), jnp.stack(kva_s)
    heads_a = lambda t: t.reshape(t.shape[:-1] + (N_KV_A, HEAD_DIM))
    heads_b = lambda t: t.reshape(t.shape[:-1] + (N_HEADS_B, HEAD_DIM))
    ka_p, va_p = heads_a(kva_p[..., :LANES]), heads_a(kva_p[..., LANES:])
    kb_p, vb_p = heads_b(jnp.stack(kb_p)), heads_b(jnp.stack(vb_p))
    ka_s = heads_a(kva_s[..., :LANES].reshape(DEPTH, DEC_BATCH, DEC_SEQ, LANES))
    va_s = heads_a(kva_s[..., LANES:].reshape(DEPTH, DEC_BATCH, DEC_SEQ, LANES))
    keep_a = min(WINDOW_A, la + DEC_SEQ)
    new_ak = jnp.concatenate([cache_a_k, ka_s], axis=2)[:, :, -keep_a:]
    new_av = jnp.concatenate([cache_a_v, va_s], axis=2)[:, :, -keep_a:]
    new_bk = from_t(_cache_shift_call(cbk_t, jnp.stack(kb_s)))
    new_bv = from_t(_cache_shift_call(cbv_t, jnp.stack(vb_s)))
    return (y_prompt, y_sample, ka_p, va_p, kb_p, vb_p, new_ak, new_av, new_bk, new_bv)
```

```python
import functools
import math

import jax
import jax.numpy as jnp
from jax import lax
from jax.experimental import pallas as pl
from jax.experimental.pallas import tpu as pltpu

F32 = jnp.float32
BF16 = jnp.bfloat16

D_MODEL = 2048
BATCH = 2
SEQ = 4096
DEPTH = 4
DEC_BATCH = 32
DEC_SEQ = 8
PAST_LEN = 16384
HEAD_DIM = 64
N_HEADS_A = 16
N_KV_A = 2
N_HEADS_B = 16
WINDOW_A = 128
DILATED = ((128, 1), (512, 4), (2048, 16))
WINDOW_B = 2048
BLOCK = 128
NUM_BUCKETS = 32
MAX_DISTANCE = 2048
ATTN_SCALE = HEAD_DIM ** -0.5
NEG_INF = -1e30
D_FF = 5632
N_EXPERTS = 8
TOP_K = 2
D_FF_EXPERT = 7168
RMS_EPS = 1e-5

N_PROMPT = BATCH * SEQ
N_SAMPLE = DEC_BATCH * DEC_SEQ
N_TOK = N_PROMPT + N_SAMPLE
WIDTH_A = N_HEADS_A * HEAD_DIM
WIDTH_KVA = 2 * N_KV_A * HEAD_DIM
WIDTH_B = N_HEADS_B * HEAD_DIM
LANES = 128
ROW_CHUNKS = D_MODEL // LANES
MOE_ROWS = 256
N_ASSIGN = TOP_K * N_TOK
N_MOE_BLOCKS = N_ASSIGN // MOE_ROWS + N_EXPERTS
N_MOE_ROWS = N_MOE_BLOCKS * MOE_ROWS
VMEM_LIMIT = 56 * 1024 * 1024

TM = 768
CAST_ROWS = 256


def _params(sem):
    return pltpu.CompilerParams(dimension_semantics=sem, vmem_limit_bytes=VMEM_LIMIT)


def _cast_weight(w_ref, wbf_ref):
    steps = w_ref.shape[0] // CAST_ROWS

    def body(i, c):
        rows = pl.ds(pl.multiple_of(i * CAST_ROWS, CAST_ROWS), CAST_ROWS)
        wbf_ref[rows, :] = w_ref[rows, :].astype(BF16)
        return c

    lax.fori_loop(0, steps, body, 0)


def _rmsnorm_rows(x, g):
    ms = jnp.mean(x * x, axis=-1, keepdims=True)
    return x * lax.rsqrt(ms + RMS_EPS) * g


def _norm_kernel(x_ref, g_ref, o_ref):
    o_ref[...] = _rmsnorm_rows(x_ref[...], g_ref[...]).astype(o_ref.dtype)


def _norm_call(x, g, out_dtype, tm=256):
    rows, width = x.shape
    return pl.pallas_call(
        _norm_kernel,
        grid=(rows // tm,),
        in_specs=[pl.BlockSpec((tm, width), lambda i: (i, 0)),
                  pl.BlockSpec((1, width), lambda i: (0, 0))],
        out_specs=pl.BlockSpec((tm, width), lambda i: (i, 0)),
        out_shape=jax.ShapeDtypeStruct((rows, width), out_dtype),
        compiler_params=_params(("arbitrary",)),
        name="rmsnorm",
    )(x, g.reshape(1, width))


QKV_WIDTHS = (WIDTH_A, WIDTH_KVA, WIDTH_B, WIDTH_B, WIDTH_B)
QKV_DIM = sum(QKV_WIDTHS)
TM_QKV = 384
TN_QKV = 512


def _qkv_kernel(x_ref, g_ref, w_ref, *o_refs):
    xn = _rmsnorm_rows(x_ref[...], g_ref[...]).astype(BF16)
    col = 0
    for o_ref in o_refs:
        width = o_ref.shape[1]
        for c in range(0, width, TN_QKV):
            cols = min(TN_QKV, width - c)
            o_ref[:, c:c + cols] = jnp.dot(xn, w_ref[:, col + c:col + c + cols], preferred_element_type=F32)
        col += width


def _to_bf16_kernel(w_ref, o_ref):
    o_ref[...] = w_ref[...].astype(BF16)


def _to_bf16_call(w, layer, rows=256):
    k, n = w.shape[1:]
    return pl.pallas_call(
        _to_bf16_kernel,
        grid=(k // rows,),
        in_specs=[pl.BlockSpec((None, rows, n), lambda i: (layer, i, 0))],
        out_specs=pl.BlockSpec((rows, n), lambda i: (i, 0)),
        out_shape=jax.ShapeDtypeStruct((k, n), BF16),
        compiler_params=_params(("arbitrary",)),
        name="weight_to_bf16",
    )(w)


def _qkv_call(x, g, w_in, layer):
    w_bf = _to_bf16_call(w_in, layer)
    return pl.pallas_call(
        _qkv_kernel,
        grid=(N_TOK // TM_QKV,),
        in_specs=[pl.BlockSpec((TM_QKV, D_MODEL), lambda m: (m, 0)),
                  pl.BlockSpec((1, D_MODEL), lambda m: (0, 0)),
                  pl.BlockSpec((D_MODEL, QKV_DIM), lambda m: (0, 0), pipeline_mode=pl.Buffered(1))],
        out_specs=[pl.BlockSpec((TM_QKV, w), lambda m: (m, 0)) for w in QKV_WIDTHS],
        out_shape=[jax.ShapeDtypeStruct((N_TOK, w), F32) for w in QKV_WIDTHS],
        compiler_params=_params(("arbitrary",)),
        name="qkv_proj",
    )(x, g.reshape(1, D_MODEL), w_bf)


def _mm_res_kernel(a_ref, w_ref, r_ref, o_ref, wbf_ref):
    @pl.when(pl.program_id(1) == 0)
    def _():
        _cast_weight(w_ref, wbf_ref)

    o_ref[...] = r_ref[...] + jnp.dot(a_ref[...], wbf_ref[...], preferred_element_type=F32)


def _mm_res_call(a, w, layer, res, tm, tn, name):
    m_rows, k = a.shape
    n_cols = w.shape[2]
    return pl.pallas_call(
        _mm_res_kernel,
        grid=(n_cols // tn, m_rows // tm),
        in_specs=[pl.BlockSpec((tm, k), lambda n, m: (m, 0)),
                  pl.BlockSpec((None, k, tn), lambda n, m: (layer, 0, n)),
                  pl.BlockSpec((tm, tn), lambda n, m: (m, n))],
        out_specs=pl.BlockSpec((tm, tn), lambda n, m: (m, n)),
        out_shape=jax.ShapeDtypeStruct((m_rows, n_cols), F32),
        scratch_shapes=[pltpu.VMEM((k, tn), BF16)],
        compiler_params=_params(("arbitrary", "arbitrary")),
        name=name,
    )(a, w, res)


def _silu_mul(g, u):
    return g * (1.0 / (1.0 + jnp.exp(-g))) * u


def _swiglu_kernel(a_ref, wg_ref, wu_ref, o_ref, wgbf_ref, wubf_ref):
    @pl.when(pl.program_id(1) == 0)
    def _():
        _cast_weight(wg_ref, wgbf_ref)
        _cast_weight(wu_ref, wubf_ref)

    a = a_ref[...]
    g = jnp.dot(a, wgbf_ref[...], preferred_element_type=F32)
    u = jnp.dot(a, wubf_ref[...], preferred_element_type=F32)
    o_ref[...] = _silu_mul(g, u).astype(o_ref.dtype)


def _swiglu_call(a, wg, wu, idx, tm=TM, tn=512):
    m_rows, k = a.shape
    n_cols = wg.shape[2]
    wspec = pl.BlockSpec((None, k, tn), lambda n, m: (idx, 0, n))
    return pl.pallas_call(
        _swiglu_kernel,
        grid=(n_cols // tn, m_rows // tm),
        in_specs=[pl.BlockSpec((tm, k), lambda n, m: (m, 0)), wspec, wspec],
        out_specs=pl.BlockSpec((tm, tn), lambda n, m: (m, n)),
        out_shape=jax.ShapeDtypeStruct((m_rows, n_cols), BF16),
        scratch_shapes=[pltpu.VMEM((k, tn), BF16), pltpu.VMEM((k, tn), BF16)],
        compiler_params=_params(("arbitrary", "arbitrary")),
        name="dense_gate_up",
    )(a, wg, wu)


def _t5_bucket_idx(dist):
    n = jnp.maximum(dist, 0)
    max_exact = NUM_BUCKETS // 2
    ratio = jnp.log(jnp.maximum(n, max_exact).astype(F32) / max_exact) / math.log(MAX_DISTANCE / max_exact)
    large = max_exact + (ratio * (NUM_BUCKETS - max_exact)).astype(jnp.int32)
    large = jnp.minimum(large, NUM_BUCKETS - 1)
    return jnp.where(n < max_exact, n, large)


def _bias_kernel(idx_ref, tab_ref, o_ref):
    idx = idx_ref[...]
    acc = jnp.full(idx.shape, NEG_INF, F32)
    for b in range(NUM_BUCKETS):
        acc = jnp.where(idx == b, tab_ref[b:b + 1, :], acc)
    o_ref[...] = acc


def _bias_call(idx, tab_rows):
    groups = tab_rows.shape[0]
    rows, cols = idx.shape
    return pl.pallas_call(
        _bias_kernel,
        grid=(groups,),
        in_specs=[pl.BlockSpec((rows, cols), lambda g: (0, 0)),
                  pl.BlockSpec((None, NUM_BUCKETS, cols), lambda g: (g, 0, 0))],
        out_specs=pl.BlockSpec((None, rows, cols), lambda g: (g, 0, 0)),
        out_shape=jax.ShapeDtypeStruct((groups, rows, cols), F32),
        compiler_params=_params(("arbitrary",)),
        name="rel_bias",
    )(idx, tab_rows)


def _band_bucket_idx(step, max_dist):
    i = lax.broadcasted_iota(jnp.int32, (BLOCK, 2 * BLOCK), 0)
    c = lax.broadcasted_iota(jnp.int32, (BLOCK, 2 * BLOCK), 1)
    dist = (i - c + BLOCK) * step
    valid = (dist >= 0) & (dist <= max_dist)
    return jnp.where(valid, _t5_bucket_idx(dist), -1)


def _dilation_multiplicity(dist):
    mult = jnp.zeros(dist.shape, jnp.int32)
    for w, r in DILATED:
        mult = mult + ((dist >= 0) & (dist <= w) & (dist % r == 0)).astype(jnp.int32)
    return mult


def _lane_half_mask():
    return lax.broadcasted_iota(jnp.int32, (BLOCK, LANES), 1) < HEAD_DIM


def _dup_head(t, kv):
    rolled = pltpu.roll(t, HEAD_DIM, axis=1)
    low = lax.broadcasted_iota(jnp.int32, t.shape, 1) < HEAD_DIM
    return jnp.where(low, t, rolled) if kv == 0 else jnp.where(low, rolled, t)


def _nt_dot(a, b):
    return lax.dot_general(a, b, (((1,), (1,)), ((), ())), preferred_element_type=F32)


def _band_scores(q2, kp, kc, bias, heads, first):
    low = _lane_half_mask()
    out = []
    for half, h in enumerate(heads):
        qh = jnp.where(low if half == 0 else ~low, q2, 0.0).astype(BF16)
        bias_prev = bias[h, :, 0:BLOCK]
        if first is not False:
            bias_prev = jnp.where(first, NEG_INF, bias_prev)
        out.append((_nt_dot(qh, kp) + bias_prev, _nt_dot(qh, kc) + bias[h, :, BLOCK:2 * BLOCK]))
    return out


def _band_weights(scores, heads, sink_ref):
    out = []
    for (sp, sc), h in zip(scores, heads):
        m = jnp.max(jnp.maximum(sp, sc), axis=1, keepdims=True)
        if sink_ref is not None:
            m = jnp.maximum(m, sink_ref[h])
        pp = jnp.exp(sp - m)
        pc = jnp.exp(sc - m)
        l = jnp.sum(pp + pc, axis=1, keepdims=True)
        if sink_ref is not None:
            l = l + jnp.exp(sink_ref[h] - m)
        out.append((m, l, pp.astype(BF16), pc.astype(BF16)))
    return out


def _band_values(weights, vp, vc):
    low = _lane_half_mask()
    os_ = [jnp.dot(pp, vp, preferred_element_type=F32) + jnp.dot(pc, vc, preferred_element_type=F32)
           for _, _, pp, pc in weights]
    pick = lambda a, b: jnp.where(low, a, b)
    return pick(weights[0][0], weights[1][0]), pick(weights[0][1], weights[1][1]), pick(os_[0], os_[1])


SCORE_LOOKAHEAD = 2


def _pipelined_pairs(n_items, scores_of, values_of, finish, sink_ref=None):
    pending = {}
    for i in range(n_items + SCORE_LOOKAHEAD):
        if i < n_items:
            q2, kp, kc, bias, heads, first = scores_of(i)
            pending[i] = (heads, _band_scores(q2, kp, kc, bias, heads, first))
        j = i - SCORE_LOOKAHEAD
        if j >= 0:
            heads, scores = pending.pop(j)
            finish(j, *_band_values(_band_weights(scores, heads, sink_ref), *values_of(j)))


def _swa_kernel(sink_ref, q_ref, kvp_ref, kvc_ref, bm_ref, o_ref):
    first = pl.program_id(1) == 0
    dup = lambda ref, cols: [_dup_head(ref[:, cols], kv).astype(BF16) for kv in range(N_KV_A)]
    kprev, kcur = dup(kvp_ref, slice(0, LANES)), dup(kvc_ref, slice(0, LANES))
    vprev, vcur = dup(kvp_ref, slice(LANES, 2 * LANES)), dup(kvc_ref, slice(LANES, 2 * LANES))
    cols = lambda p: slice(p * LANES, (p + 1) * LANES)
    kv_of = lambda p: (2 * p) // (N_HEADS_A // N_KV_A)

    def scores_of(p):
        return q_ref[:, cols(p)] * ATTN_SCALE, kprev[kv_of(p)], kcur[kv_of(p)], bm_ref, (2 * p, 2 * p + 1), first

    def finish(p, m, l, o):
        o_ref[:, cols(p)] = o / l

    _pipelined_pairs(WIDTH_A // LANES, scores_of, lambda p: (vprev[kv_of(p)], vcur[kv_of(p)]), finish, sink_ref)


def _swa_prompt_call(qa, kva, bm, sink):
    nb = SEQ // BLOCK
    cur = lambda b, n: (b * nb + n, 0)
    prev = lambda b, n: (b * nb + jnp.maximum(n - 1, 0), 0)
    return pl.pallas_call(
        _swa_kernel,
        grid=(BATCH, nb),
        in_specs=[pl.BlockSpec(memory_space=pltpu.SMEM),
                  pl.BlockSpec((BLOCK, WIDTH_A), cur),
                  pl.BlockSpec((BLOCK, WIDTH_KVA), prev),
                  pl.BlockSpec((BLOCK, WIDTH_KVA), cur),
                  pl.BlockSpec((N_HEADS_A, BLOCK, 2 * BLOCK), lambda b, n: (0, 0, 0))],
        out_specs=pl.BlockSpec((BLOCK, WIDTH_A), cur),
        out_shape=jax.ShapeDtypeStruct((N_PROMPT, WIDTH_A), F32),
        compiler_params=_params(("arbitrary", "arbitrary")),
        name="swa_prompt",
    )(sink, qa, kva, kva, bm)


SUPER = BLOCK * DILATED[-1][1]
HEADS_PER_STEP = 2
STEP_WIDTH = HEADS_PER_STEP * HEAD_DIM
BLOCKS_PER_STEP = 4
SUBLANES = 8


def _dilated_kernel(q_ref, k_ref, v_ref, bm_ref, o_ref, kwin, vwin, m_s, l_s, acc_s):
    n = pl.program_id(2)

    @pl.when(n == 0)
    def _():
        kwin[0:SUPER, :] = jnp.zeros((SUPER, STEP_WIDTH), F32)
        vwin[0:SUPER, :] = jnp.zeros((SUPER, STEP_WIDTH), F32)

    @pl.when(n > 0)
    def _():
        kwin[0:SUPER, :] = kwin[SUPER:2 * SUPER, :]
        vwin[0:SUPER, :] = vwin[SUPER:2 * SUPER, :]

    kwin[SUPER:2 * SUPER, :] = k_ref[...]
    vwin[SUPER:2 * SUPER, :] = v_ref[...]

    def sweep(d, r, starts, firsts):
        last = d == len(DILATED) - 1
        rows = lambda s: pl.ds(s, BLOCK, stride=r) if r > 1 else pl.ds(s, BLOCK)

        def scores_of(i):
            q2 = q_ref[rows(starts[i]), :] * ATTN_SCALE
            kp = kwin[rows(SUPER + starts[i] - BLOCK * r), :].astype(BF16)
            kc = kwin[rows(SUPER + starts[i]), :].astype(BF16)
            return q2, kp, kc, bm_ref.at[d], (0, 1), firsts[i]

        def values_of(i):
            return (vwin[rows(SUPER + starts[i] - BLOCK * r), :].astype(BF16),
                    vwin[rows(SUPER + starts[i]), :].astype(BF16))

        def finish(i, m, l, o):
            at = rows(starts[i])
            if d > 0:
                m_old = m_s[at, :]
                m_new = jnp.maximum(m_old, m)
                a_old = jnp.exp(m_old - m_new)
                a_new = jnp.exp(m - m_new)
                l = a_old * l_s[at, :] + a_new * l
                o = a_old * acc_s[at, :] + a_new * o
                m = m_new
            if last:
                o_ref[at, :] = o / l
            else:
                m_s[at, :] = m
                l_s[at, :] = l
                acc_s[at, :] = o

        _pipelined_pairs(len(starts), scores_of, values_of, finish)

    for d, (_, r) in enumerate(DILATED):
        span = BLOCK * r
        per_step = max(BLOCKS_PER_STEP, min(r, SUBLANES))
        if r >= per_step:
            groups = r // per_step
            assert groups == 1 or per_step % SUBLANES == 0
            offsets = tuple(range(per_step))
            base_of = lambda i, groups=groups, span=span, per_step=per_step: (
                pl.multiple_of((i // groups) * span + (i % groups) * per_step, SUBLANES))
            firsts_of = lambda i, groups=groups, per_step=per_step: [(n == 0) & (i // groups == 0)] * per_step
            steps = (SUPER // span) * groups
        else:
            offsets = tuple(k * span for k in range(per_step))
            base_of = lambda i, span=span, per_step=per_step: pl.multiple_of(i * span * per_step, SUBLANES)
            firsts_of = lambda i, per_step=per_step: [(n == 0) & (i == 0)] + [False] * (per_step - 1)
            steps = SUPER // (span * per_step)

        def body(i, c, d=d, r=r, offsets=offsets, base_of=base_of, firsts_of=firsts_of):
            base = base_of(i)
            sweep(d, r, [base + off for off in offsets], firsts_of(i))
            return c

        lax.fori_loop(0, steps, body, 0)


def _dilated_prompt_call(qb, kb, vb, bm):
    tiles = SEQ // SUPER
    tile = pl.BlockSpec((SUPER, STEP_WIDTH), lambda b, g, n: (b * tiles + n, g))
    return pl.pallas_call(
        _dilated_kernel,
        grid=(BATCH, N_HEADS_B // HEADS_PER_STEP, tiles),
        in_specs=[tile, tile, tile,
                  pl.BlockSpec((len(DILATED), HEADS_PER_STEP, BLOCK, 2 * BLOCK), lambda b, g, n: (0, g, 0, 0))],
        out_specs=tile,
        out_shape=jax.ShapeDtypeStruct((N_PROMPT, WIDTH_B), F32),
        scratch_shapes=[pltpu.VMEM((2 * SUPER, STEP_WIDTH), F32), pltpu.VMEM((2 * SUPER, STEP_WIDTH), F32),
                        pltpu.VMEM((SUPER, STEP_WIDTH), F32), pltpu.VMEM((SUPER, STEP_WIDTH), F32),
                        pltpu.VMEM((SUPER, STEP_WIDTH), F32)],
        compiler_params=_params(("arbitrary", "arbitrary", "arbitrary")),
        name="dilated_prompt",
    )(qb, kb, vb, bm)


def _merge_kernel(oa_ref, ob_ref, ga_ref, gb_ref, o_ref):
    o_ref[:, 0:WIDTH_A] = _rmsnorm_rows(oa_ref[...], ga_ref[...]).astype(o_ref.dtype)
    o_ref[:, WIDTH_A:] = _rmsnorm_rows(ob_ref[...], gb_ref[...]).astype(o_ref.dtype)


def _merge_call(oa, ob, ga, gb, tm=256):
    row = lambda w: pl.BlockSpec((tm, w), lambda i: (i, 0))
    vec = lambda w: pl.BlockSpec((1, w), lambda i: (0, 0))
    return pl.pallas_call(
        _merge_kernel,
        grid=(N_PROMPT // tm,),
        in_specs=[row(WIDTH_A), row(WIDTH_B), vec(WIDTH_A), vec(WIDTH_B)],
        out_specs=row(WIDTH_A + WIDTH_B),
        out_shape=jax.ShapeDtypeStruct((N_PROMPT, WIDTH_A + WIDTH_B), BF16),
        compiler_params=_params(("arbitrary",)),
        name="merge_prompt",
    )(oa, ob, ga.reshape(1, -1), gb.reshape(1, -1))


NEW_ROWS = 128
SINK_ROW = DEC_SEQ
HEAD_GROUPS = 2
GROUP_WIDTH = WIDTH_B // HEAD_GROUPS


def _pad_rows(t, rows):
    return jnp.concatenate([t, jnp.zeros((rows - t.shape[0], t.shape[1]), t.dtype)], axis=0)


def _sample_window_a(qa_ref, kva_ref, cak_ref, cav_ref, bac_ref, ban_ref, sink_ref):
    group = N_HEADS_A // N_KV_A
    lane8 = lax.broadcasted_iota(jnp.int32, (DEC_SEQ, LANES), 1) < HEAD_DIM
    rows = []
    for p in range(WIDTH_A // LANES):
        blk = qa_ref[:, p * LANES:(p + 1) * LANES] * ATTN_SCALE
        rolled = pltpu.roll(blk, HEAD_DIM, axis=1)
        if (2 * p) // group == 0:
            rows += [jnp.where(lane8, blk, 0.0), jnp.where(lane8, rolled, 0.0)]
        else:
            rows += [jnp.where(lane8, 0.0, rolled), jnp.where(lane8, 0.0, blk)]
    qat = jnp.concatenate(rows, axis=0).astype(BF16)
    kan = _pad_rows(kva_ref[:, 0:LANES], NEW_ROWS)
    van = _pad_rows(kva_ref[:, LANES:2 * LANES], NEW_ROWS)
    row_id = lax.broadcasted_iota(jnp.int32, (NEW_ROWS, LANES), 0)
    sc = _nt_dot(cak_ref[...].astype(BF16), qat) + bac_ref[...]
    sn = _nt_dot(kan.astype(BF16), qat) + jnp.where(row_id == SINK_ROW, sink_ref[...], ban_ref[...])
    m = jnp.maximum(jnp.max(sc, axis=0, keepdims=True), jnp.max(sn, axis=0, keepdims=True))
    pct = jnp.exp(sc - m).T
    pnt = jnp.exp(sn - m).T
    l_col = jnp.sum(pct, axis=1, keepdims=True) + jnp.sum(pnt, axis=1, keepdims=True)
    pct = pct.astype(BF16)
    pnt = pnt.astype(BF16)
    res = []
    for kv in range(N_KV_A):
        vc = _dup_head(cav_ref[...], kv).astype(BF16)
        vn = _dup_head(van, kv).astype(BF16)
        res.append((jnp.dot(pct, vc, preferred_element_type=F32)
                    + jnp.dot(pnt, vn, preferred_element_type=F32)) / l_col)
    pairs = []
    for p in range(WIDTH_A // LANES):
        r = res[(2 * p) // group]
        pairs.append(jnp.where(lane8, r[16 * p:16 * p + 8, :], r[16 * p + 8:16 * p + 16, :]))
    return jnp.concatenate(pairs, axis=1)


def _shift_in_new(old_ref, new_ref, out_ref):
    feats, window = old_ref.shape
    keep = LANES - DEC_SEQ
    lane = lax.broadcasted_iota(jnp.int32, (feats, LANES), 1)
    tails = []
    for c in range(feats // LANES):
        sq = _pad_rows(new_ref[:, c * LANES:(c + 1) * LANES], LANES).T
        tails.append(pltpu.roll(sq, keep, axis=1))
    nxt = jnp.concatenate(tails, axis=0)
    for j in reversed(range(window // LANES)):
        cur = pltpu.roll(old_ref[:, j * LANES:(j + 1) * LANES], keep, axis=1)
        out_ref[:, j * LANES:(j + 1) * LANES] = jnp.where(lane < keep, cur, nxt)
        nxt = cur


def _sample_kernel(qa_ref, kva_ref, qb_ref, kb_ref, vb_ref, cak_ref, cav_ref, cbk_ref, cbv_ref,
                   bac_ref, ban_ref, sink_ref, bbc_ref, bbn_ref, mbc_ref, mbn_ref, ga_ref, gb_ref,
                   o_ref, oa_s, ob_s):
    g = pl.program_id(1)

    @pl.when(g == 0)
    def _():
        oa_s[...] = _sample_window_a(qa_ref, kva_ref, cak_ref, cav_ref, bac_ref, ban_ref, sink_ref)

    low = lax.broadcasted_iota(jnp.int32, (DEC_SEQ, LANES), 1) < HEAD_DIM
    pairs = []
    for p in range(GROUP_WIDTH // LANES):
        cols = slice(p * LANES, (p + 1) * LANES)
        q2 = qb_ref[:, cols] * ATTN_SCALE
        kt = cbk_ref[cols, :].astype(BF16)
        vt = cbv_ref[cols, :].astype(BF16)
        kn = _pad_rows(kb_ref[:, cols], NEW_ROWS).astype(BF16)
        vn = _pad_rows(vb_ref[:, cols], NEW_ROWS).astype(BF16)
        halves = []
        for half in range(2):
            i = 2 * p + half
            qh = jnp.where(low if half == 0 else ~low, q2, 0.0).astype(BF16)
            sc = jnp.dot(qh, kt, preferred_element_type=F32) + bbc_ref[i]
            sn = _nt_dot(qh, kn) + bbn_ref[i]
            m = jnp.maximum(jnp.max(sc, axis=1, keepdims=True), jnp.max(sn, axis=1, keepdims=True))
            pc = jnp.exp(sc - m) * mbc_ref[...]
            pn = jnp.exp(sn - m) * mbn_ref[...]
            l = jnp.sum(pc, axis=1, keepdims=True) + jnp.sum(pn, axis=1, keepdims=True)
            o = _nt_dot(pc.astype(BF16), vt) + jnp.dot(pn.astype(BF16), vn, preferred_element_type=F32)
            halves.append(o / l)
        pairs.append(jnp.where(low, halves[0], halves[1]))
    ob = jnp.concatenate(pairs, axis=1)
    for k in range(HEAD_GROUPS):
        @pl.when(g == k)
        def _(k=k):
            ob_s[:, k * GROUP_WIDTH:(k + 1) * GROUP_WIDTH] = ob

    @pl.when(g == HEAD_GROUPS - 1)
    def _():
        o_ref[:, 0:WIDTH_A] = _rmsnorm_rows(oa_s[...], ga_ref[...])
        o_ref[:, WIDTH_A:] = _rmsnorm_rows(ob_s[...], gb_ref[...])


def _sample_call(qa, kva, qb, kb, vb, cak, cav, cbk_t, cbv_t, consts, sink_row, ga, gb, layer):
    first = N_PROMPT // DEC_SEQ
    new = lambda w: pl.BlockSpec((DEC_SEQ, w), lambda b, g: (first + b, 0))
    new_g = pl.BlockSpec((DEC_SEQ, GROUP_WIDTH), lambda b, g: (first + b, g))
    la, lb = cak.shape[2], cbk_t.shape[3]
    cache_a = pl.BlockSpec((None, None, la, LANES), lambda b, g: (layer, b, 0, 0))
    cache_b = pl.BlockSpec((None, None, GROUP_WIDTH, lb), lambda b, g: (layer, b, g, 0))
    const = lambda *shape: pl.BlockSpec(shape, lambda b, g: (0,) * len(shape))
    heads = N_HEADS_B // HEAD_GROUPS
    per_head = lambda *shape: pl.BlockSpec((heads,) + shape, lambda b, g: (g,) + (0,) * len(shape))
    in_specs = [new(WIDTH_A), new(WIDTH_KVA), new_g, new_g, new_g, cache_a, cache_a, cache_b, cache_b,
                const(la, LANES), const(NEW_ROWS, LANES), const(1, LANES),
                per_head(DEC_SEQ, lb), per_head(DEC_SEQ, NEW_ROWS), const(DEC_SEQ, lb), const(DEC_SEQ, NEW_ROWS),
                const(1, WIDTH_A), const(1, WIDTH_B)]
    args = [qa, kva, qb, kb, vb, cak, cav, cbk_t, cbv_t,
            consts["bac"], consts["ban"], sink_row, consts["bbc"], consts["bbn"], consts["mbc"], consts["mbn"],
            ga.reshape(1, -1), gb.reshape(1, -1)]
    return pl.pallas_call(
        _sample_kernel,
        grid=(DEC_BATCH, HEAD_GROUPS),
        in_specs=in_specs,
        out_specs=pl.BlockSpec((DEC_SEQ, WIDTH_A + WIDTH_B), lambda b, g: (b, 0)),
        out_shape=jax.ShapeDtypeStruct((N_SAMPLE, WIDTH_A + WIDTH_B), F32),
        scratch_shapes=[pltpu.VMEM((DEC_SEQ, WIDTH_A), F32), pltpu.VMEM((DEC_SEQ, WIDTH_B), F32)],
        compiler_params=_params(("arbitrary", "arbitrary")),
        name="sample_attention",
    )(*args)


def _cache_shift_kernel(old_ref, new_ref, out_ref):
    _shift_in_new(old_ref, new_ref, out_ref)


def _cache_shift_call(cache_t, new_rows):
    lb = cache_t.shape[3]
    block = pl.BlockSpec((None, None, GROUP_WIDTH, lb), lambda l, b, g: (l, b, g, 0))
    return pl.pallas_call(
        _cache_shift_kernel,
        grid=(DEPTH, DEC_BATCH, HEAD_GROUPS),
        in_specs=[block, pl.BlockSpec((None, DEC_SEQ, GROUP_WIDTH), lambda l, b, g: (l, b, g))],
        out_specs=block,
        out_shape=jax.ShapeDtypeStruct(cache_t.shape, F32),
        compiler_params=_params(("arbitrary", "arbitrary", "arbitrary")),
        name="cache_shift",
    )(cache_t, new_rows)


def _sample_consts(rel_bias):
    la, lb = WINDOW_A, WINDOW_B

    def idx_of(dist, valid):
        return jnp.where(valid, _t5_bucket_idx(dist), -1)

    t = lax.broadcasted_iota(jnp.int32, (1, LANES), 1) % DEC_SEQ
    tab_a = jnp.repeat(rel_bias[:, :N_HEADS_A], DEC_SEQ, axis=1)[None]
    i_a = lax.broadcasted_iota(jnp.int32, (la, LANES), 0)
    dist = la + t - i_a
    bac = _bias_call(idx_of(dist, dist <= WINDOW_A - 1), tab_a)[0]
    j = lax.broadcasted_iota(jnp.int32, (NEW_ROWS, LANES), 0)
    dist_n = t - j
    ban = _bias_call(idx_of(dist_n, (dist_n >= 0) & (j < DEC_SEQ)), tab_a)[0]
    tab_b = rel_bias[:, N_HEADS_A:].T[:, :, None]
    tq = lax.broadcasted_iota(jnp.int32, (DEC_SEQ, lb), 0)
    ik = lax.broadcasted_iota(jnp.int32, (DEC_SEQ, lb), 1)
    dist_b = lb + tq - ik
    mult_c = _dilation_multiplicity(dist_b)
    bbc = _bias_call(idx_of(dist_b, mult_c > 0), jnp.broadcast_to(tab_b, (N_HEADS_B, NUM_BUCKETS, lb)))
    tn = lax.broadcasted_iota(jnp.int32, (DEC_SEQ, NEW_ROWS), 0)
    jn = lax.broadcasted_iota(jnp.int32, (DEC_SEQ, NEW_ROWS), 1)
    mult_n = jnp.where(jn < DEC_SEQ, _dilation_multiplicity(tn - jn), 0)
    bbn = _bias_call(idx_of(tn - jn, mult_n > 0), jnp.broadcast_to(tab_b, (N_HEADS_B, NUM_BUCKETS, NEW_ROWS)))
    return {"bac": bac, "ban": ban, "bbc": bbc, "bbn": bbn,
            "mbc": mult_c.astype(F32), "mbn": mult_n.astype(F32)}


def _router_kernel(x_ref, rt_ref, idx_ref, gate_ref):
    logits = lax.dot_general(rt_ref[...], x_ref[...], (((1,), (1,)), ((), ())),
                             precision=lax.Precision.HIGHEST, preferred_element_type=F32)
    e_id = lax.broadcasted_iota(jnp.int32, logits.shape, 0)
    m1 = jnp.max(logits, axis=0, keepdims=True)
    i1 = jnp.min(jnp.where(logits == m1, e_id, N_EXPERTS), axis=0, keepdims=True)
    rest = jnp.where(e_id == i1, -jnp.inf, logits)
    m2 = jnp.max(rest, axis=0, keepdims=True)
    i2 = jnp.min(jnp.where(rest == m2, e_id, N_EXPERTS), axis=0, keepdims=True)
    e2 = jnp.exp(m2 - m1)
    den = 1.0 + e2
    idx_ref[0:1, :] = i1
    idx_ref[1:2, :] = i2
    gate_ref[0:1, :] = 1.0 / den
    gate_ref[1:2, :] = e2 / den


def _router_call(xn, router_t, tm=TM):
    return pl.pallas_call(
        _router_kernel,
        grid=(N_TOK // tm,),
        in_specs=[pl.BlockSpec((tm, D_MODEL), lambda i: (i, 0)),
                  pl.BlockSpec((N_EXPERTS, D_MODEL), lambda i: (0, 0))],
        out_specs=[pl.BlockSpec((TOP_K, tm), lambda i: (0, i)),
                   pl.BlockSpec((TOP_K, tm), lambda i: (0, i))],
        out_shape=[jax.ShapeDtypeStruct((TOP_K, N_TOK), jnp.int32),
                   jax.ShapeDtypeStruct((TOP_K, N_TOK), F32)],
        compiler_params=_params(("arbitrary",)),
        name="moe_router",
    )(xn, router_t)


def _routing_plan(top_i):
    e = top_i.T.reshape(-1)
    order = jnp.argsort(e, stable=True)
    onehot = (e[:, None] == jnp.arange(N_EXPERTS)[None, :]).astype(jnp.int32)
    counts = jnp.sum(onehot, axis=0)
    starts = jnp.cumsum(counts) - counts
    padded = (counts + MOE_ROWS - 1) // MOE_ROWS * MOE_ROWS
    pstarts = jnp.cumsum(padded) - padded
    blk_start = jnp.arange(N_MOE_BLOCKS) * MOE_ROWS
    blk_e = jnp.minimum(jnp.sum(blk_start[:, None] >= (pstarts + padded)[None, :], axis=1), N_EXPERTS - 1)
    n_used = (jnp.sum(padded) // MOE_ROWS).astype(jnp.int32).reshape(1)
    row = jnp.arange(N_MOE_ROWS)
    row_e = blk_e[row // MOE_ROWS]
    within = row - pstarts[row_e]
    src = order[jnp.clip(starts[row_e] + within, 0, N_ASSIGN - 1)]
    row_tok = jnp.where(within < counts[row_e], src // TOP_K, 0).astype(jnp.int32)
    rank = jnp.take_along_axis(jnp.cumsum(onehot, axis=0) - onehot, e[:, None], axis=1)[:, 0]
    dest = (pstarts[e] + rank).astype(jnp.int32)
    dest_kt = dest.reshape(N_TOK, TOP_K).T.reshape(-1)
    return blk_e.astype(jnp.int32), n_used, row_tok, dest_kt


def _row_copy(src_hbm, row, dst, slot, sem):
    return pltpu.make_async_copy(
        src_hbm.at[pl.ds(pl.multiple_of(row * ROW_CHUNKS, ROW_CHUNKS), ROW_CHUNKS), :],
        dst.at[pl.ds(pl.multiple_of(slot * ROW_CHUNKS, ROW_CHUNKS), ROW_CHUNKS), :],
        sem)


def _wait_rows(src_hbm, dst, sem):
    pltpu.make_async_copy(src_hbm.at[pl.ds(0, dst.shape[0]), :], dst, sem).wait()


ISSUE_UNROLL = 8


def _gather_kernel(tok_ref, x_hbm, o_ref, buf0, buf1, sems):
    b = pl.program_id(0)
    bufs = (buf0, buf1)

    def request(block, slot):
        def issue(i, c):
            for u in range(ISSUE_UNROLL):
                row = i * ISSUE_UNROLL + u
                _row_copy(x_hbm, tok_ref[block * MOE_ROWS + row], bufs[slot], row, sems.at[slot]).start()
            return c

        lax.fori_loop(0, MOE_ROWS // ISSUE_UNROLL, issue, 0)

    @pl.when(b == 0)
    def _():
        request(0, 0)

    for slot in range(2):
        @pl.when((b + 1 < N_MOE_BLOCKS) & ((b + 1) % 2 == slot))
        def _(slot=slot):
            request(b + 1, slot)

    for slot in range(2):
        @pl.when(b % 2 == slot)
        def _(slot=slot):
            _wait_rows(x_hbm, bufs[slot], sems.at[slot])
            for j in range(ROW_CHUNKS):
                chunk = bufs[slot][pl.ds(j, MOE_ROWS, stride=ROW_CHUNKS), :]
                o_ref[:, j * LANES:(j + 1) * LANES] = chunk.astype(o_ref.dtype)


def _gather_call(row_tok, x_chunks):
    grid_spec = pltpu.PrefetchScalarGridSpec(
        num_scalar_prefetch=1,
        grid=(N_MOE_BLOCKS,),
        in_specs=[pl.BlockSpec(memory_space=pl.ANY)],
        out_specs=pl.BlockSpec((MOE_ROWS, D_MODEL), lambda b, tok: (b, 0)),
        scratch_shapes=[pltpu.VMEM((MOE_ROWS * ROW_CHUNKS, LANES), F32),
                        pltpu.VMEM((MOE_ROWS * ROW_CHUNKS, LANES), F32),
                        pltpu.SemaphoreType.DMA((2,))],
    )
    return pl.pallas_call(
        _gather_kernel,
        grid_spec=grid_spec,
        out_shape=jax.ShapeDtypeStruct((N_MOE_ROWS, D_MODEL), BF16),
        compiler_params=_params(("arbitrary",)),
        name="moe_dispatch",
    )(row_tok, x_chunks)


def _expert_changed(be_ref, nu_ref):
    b = pl.program_id(1)
    last = jnp.minimum(b, nu_ref[0] - 1)
    return (b < nu_ref[0]) & ((b == 0) | (be_ref[last] != be_ref[jnp.maximum(last - 1, 0)]))


def _moe_up_kernel(be_ref, nu_ref, a_ref, wg_ref, wu_ref, o_ref, wgbf_ref, wubf_ref):
    @pl.when(_expert_changed(be_ref, nu_ref))
    def _():
        _cast_weight(wg_ref, wgbf_ref)
        _cast_weight(wu_ref, wubf_ref)

    used = pl.program_id(1) < nu_ref[0]

    @pl.when(used)
    def _():
        a = a_ref[...]
        g = jnp.dot(a, wgbf_ref[...], preferred_element_type=F32)
        u = jnp.dot(a, wubf_ref[...], preferred_element_type=F32)
        o_ref[...] = _silu_mul(g, u).astype(o_ref.dtype)

    @pl.when(jnp.logical_not(used))
    def _():
        o_ref[...] = jnp.zeros(o_ref.shape, o_ref.dtype)


def _moe_down_kernel(be_ref, nu_ref, a_ref, w_ref, o_ref, wbf_ref):
    @pl.when(_expert_changed(be_ref, nu_ref))
    def _():
        _cast_weight(w_ref, wbf_ref)

    used = pl.program_id(1) < nu_ref[0]

    @pl.when(used)
    def _():
        o_ref[...] = jnp.dot(a_ref[...], wbf_ref[...], preferred_element_type=F32)

    @pl.when(jnp.logical_not(used))
    def _():
        o_ref[...] = jnp.zeros(o_ref.shape, o_ref.dtype)


def _moe_matmul_call(kernel, blk_e, n_used, a, weights, idx, tn, out_dtype, name):
    k = a.shape[1]
    n_cols = weights[0].shape[3]
    last = lambda b, nu: jnp.minimum(b, nu[0] - 1)
    wspec = pl.BlockSpec((None, None, k, tn), lambda n, b, be, nu: (idx, be[last(b, nu)], 0, n))
    grid_spec = pltpu.PrefetchScalarGridSpec(
        num_scalar_prefetch=2,
        grid=(n_cols // tn, N_MOE_BLOCKS),
        in_specs=[pl.BlockSpec((MOE_ROWS, k), lambda n, b, be, nu: (last(b, nu), 0))] + [wspec] * len(weights),
        out_specs=pl.BlockSpec((MOE_ROWS, tn), lambda n, b, be, nu: (b, n)),
        scratch_shapes=[pltpu.VMEM((k, tn), BF16)] * len(weights),
    )
    return pl.pallas_call(
        kernel,
        grid_spec=grid_spec,
        out_shape=jax.ShapeDtypeStruct((N_MOE_ROWS, n_cols), out_dtype),
        compiler_params=_params(("arbitrary", "arbitrary")),
        name=name,
    )(blk_e, n_used, a, *weights)


COMBINE_ROWS = 256


def _combine_kernel(dest_ref, x_ref, g_ref, y_hbm, o_ref, buf0, buf1, sems):
    base = pl.program_id(0) * COMBINE_ROWS
    bufs = (buf0, buf1)

    def issue(i, c):
        for u in range(ISSUE_UNROLL):
            row = i * ISSUE_UNROLL + u
            for k in range(TOP_K):
                _row_copy(y_hbm, dest_ref[k * N_TOK + base + row], bufs[k], row, sems.at[k]).start()
        return c

    lax.fori_loop(0, COMBINE_ROWS // ISSUE_UNROLL, issue, 0)
    for k in range(TOP_K):
        _wait_rows(y_hbm, bufs[k], sems.at[k])
    g0 = g_ref[:, 0:1]
    g1 = g_ref[:, 1:2]
    for j in range(ROW_CHUNKS):
        cols = slice(j * LANES, (j + 1) * LANES)
        rows = pl.ds(j, COMBINE_ROWS, stride=ROW_CHUNKS)
        o_ref[:, cols] = x_ref[:, cols] + (g0 * buf0[rows, :] + g1 * buf1[rows, :])


def _combine_call(dest_kt, x, gates, y_chunks):
    grid_spec = pltpu.PrefetchScalarGridSpec(
        num_scalar_prefetch=1,
        grid=(N_TOK // COMBINE_ROWS,),
        in_specs=[pl.BlockSpec((COMBINE_ROWS, D_MODEL), lambda i, d: (i, 0)),
                  pl.BlockSpec((COMBINE_ROWS, TOP_K), lambda i, d: (i, 0)),
                  pl.BlockSpec(memory_space=pl.ANY)],
        out_specs=pl.BlockSpec((COMBINE_ROWS, D_MODEL), lambda i, d: (i, 0)),
        scratch_shapes=[pltpu.VMEM((COMBINE_ROWS * ROW_CHUNKS, LANES), F32),
                        pltpu.VMEM((COMBINE_ROWS * ROW_CHUNKS, LANES), F32),
                        pltpu.SemaphoreType.DMA((TOP_K,))],
    )
    return pl.pallas_call(
        _combine_kernel,
        grid_spec=grid_spec,
        out_shape=jax.ShapeDtypeStruct((N_TOK, D_MODEL), F32),
        compiler_params=_params(("arbitrary",)),
        name="moe_combine",
    )(dest_kt, x, gates, y_chunks)


def _moe_layer(x, g, router, wg, wu, wd, idx):
    xn = _norm_call(x, g, F32)
    top_i, gates = _router_call(xn, router[idx].T)
    blk_e, n_used, row_tok, dest_kt = _routing_plan(top_i)
    xs = _gather_call(row_tok, xn.reshape(N_TOK * ROW_CHUNKS, LANES))
    h = _moe_matmul_call(_moe_up_kernel, blk_e, n_used, xs, (wg, wu), idx, 1024, BF16, "moe_gate_up")
    y = _moe_matmul_call(_moe_down_kernel, blk_e, n_used, h, (wd,), idx, 512, F32, "moe_down")
    return _combine_call(dest_kt, x, gates.T, y.reshape(N_MOE_ROWS * ROW_CHUNKS, LANES))


def kernel(x_prompt, x_sample, cache_a_k, cache_a_v, cache_b_k, cache_b_v, g_mix_in, w_in, sinks, rel_bias,
           g_out_a, g_out_b, w_out, g_ffn, w_gate_d, w_up_d, w_down_d, router, w_gate_e, w_up_e, w_down_e,
           g_final):
    la, lb = cache_a_k.shape[2], cache_b_k.shape[2]
    x = jnp.concatenate([x_prompt.reshape(N_PROMPT, D_MODEL), x_sample.reshape(N_SAMPLE, D_MODEL)], axis=0)
    cak = cache_a_k.reshape(DEPTH, DEC_BATCH, la, N_KV_A * HEAD_DIM)
    cav = cache_a_v.reshape(DEPTH, DEC_BATCH, la, N_KV_A * HEAD_DIM)
    to_t = lambda c: jnp.transpose(c, (0, 1, 3, 4, 2)).reshape(DEPTH, DEC_BATCH, WIDTH_B, lb)
    from_t = lambda c: jnp.transpose(c.reshape(DEPTH, DEC_BATCH, N_HEADS_B, HEAD_DIM, lb), (0, 1, 4, 2, 3))
    cbk_t, cbv_t = to_t(cache_b_k), to_t(cache_b_v)

    head_rows = lambda tab: jnp.broadcast_to(tab.T[:, :, None], (tab.shape[1], NUM_BUCKETS, 2 * BLOCK))
    tab_a, tab_b = rel_bias[:, :N_HEADS_A], rel_bias[:, N_HEADS_A:]
    bm_a = _bias_call(_band_bucket_idx(1, WINDOW_A - 1), head_rows(tab_a))
    bm_b = jnp.stack([_bias_call(_band_bucket_idx(r, w), head_rows(tab_b)) for w, r in DILATED])
    consts = _sample_consts(rel_bias)

    tail = lambda t, keep: jnp.stack([t[(b + 1) * SEQ - keep:(b + 1) * SEQ] for b in range(BATCH)])
    kva_p, kb_p, vb_p, kva_s, kb_s, vb_s = [], [], [], [], [], []
    for l in range(DEPTH):
        qa, kva, qb, kb, vb = _qkv_call(x, g_mix_in[l], w_in, l)
        oa = _swa_prompt_call(qa, kva, bm_a, sinks[l])
        ob = _dilated_prompt_call(qb, kb, vb, bm_b)
        y_prompt = _merge_call(oa, ob, g_out_a[l], g_out_b[l])
        sink_row = jnp.repeat(sinks[l], DEC_SEQ).reshape(1, LANES)
        y_sample = _sample_call(qa, kva, qb, kb, vb, cak, cav, cbk_t, cbv_t, consts, sink_row,
                                g_out_a[l], g_out_b[l], l)
        y = jnp.concatenate([y_prompt, y_sample.astype(BF16)], axis=0)
        x = _mm_res_call(y, w_out, l, x, TM, 512, "out_proj")
        if l % 2 == 0:
            hn = _norm_call(x, g_ffn[l], BF16)
            h = _swiglu_call(hn, w_gate_d, w_up_d, l // 2)
            x = _mm_res_call(h, w_down_d, l // 2, x, 384, 512, "dense_down")
        else:
            x = _moe_layer(x, g_ffn[l], router, w_gate_e, w_up_e, w_down_e, l // 2)
        kva_p.append(tail(kva, min(WINDOW_A, SEQ)))
        kb_p.append(tail(kb, min(WINDOW_B, SEQ)))
        vb_p.append(tail(vb, min(WINDOW_B, SEQ)))
        kva_s.append(kva[N_PROMPT:])
        kb_s.append(kb[N_PROMPT:])
        vb_s.append(vb[N_PROMPT:])

    y = _norm_call(x, g_final, F32)
    y_prompt = y[:N_PROMPT].reshape(BATCH, SEQ, D_MODEL)
    y_sample = y[N_PROMPT:].reshape(DEC_BATCH, DEC_SEQ, D_MODEL)

    kva_p, kva_s = jnp.stack(kva_p), jnp.stack(kva_s)
    heads_a = lambda t: t.reshape(t.shape[:-1] + (N_KV_A, HEAD_DIM))
    heads_b = lambda t: t.reshape(t.shape[:-1] + (N_HEADS_B, HEAD_DIM))
    ka_p, va_p = heads_a(kva_p[..., :LANES]), heads_a(kva_p[..., LANES:])
    kb_p, vb_p = heads_b(jnp.stack(kb_p)), heads_b(jnp.stack(vb_p))
    ka_s = heads_a(kva_s[..., :LANES].reshape(DEPTH, DEC_BATCH, DEC_SEQ, LANES))
    va_s = heads_a(kva_s[..., LANES:].reshape(DEPTH, DEC_BATCH, DEC_SEQ, LANES))
    keep_a = min(WINDOW_A, la + DEC_SEQ)
    new_ak = jnp.concatenate([cache_a_k, ka_s], axis=2)[:, :, -keep_a:]
    new_av = jnp.concatenate([cache_a_v, va_s], axis=2)[:, :, -keep_a:]
    new_bk = from_t(_cache_shift_call(cbk_t, jnp.stack(kb_s)))
    new_bv = from_t(_cache_shift_call(cbv_t, jnp.stack(vb_s)))
    return (y_prompt, y_sample, ka_p, va_p, kb_p, vb_p, new_ak, new_av, new_bk, new_bv)
```

```python
import functools
import math

import jax
import jax.numpy as jnp
from jax import lax
from jax.experimental import pallas as pl
from jax.experimental.pallas import tpu as pltpu

F32 = jnp.float32
BF16 = jnp.bfloat16

D_MODEL = 2048
BATCH = 2
SEQ = 4096
DEPTH = 4
DEC_BATCH = 32
DEC_SEQ = 8
PAST_LEN = 16384
HEAD_DIM = 64
N_HEADS_A = 16
N_KV_A = 2
N_HEADS_B = 16
WINDOW_A = 128
DILATED = ((128, 1), (512, 4), (2048, 16))
WINDOW_B = 2048
BLOCK = 128
NUM_BUCKETS = 32
MAX_DISTANCE = 2048
ATTN_SCALE = HEAD_DIM ** -0.5
NEG_INF = -1e30
D_FF = 5632
N_EXPERTS = 8
TOP_K = 2
D_FF_EXPERT = 7168
RMS_EPS = 1e-5

N_PROMPT = BATCH * SEQ
N_SAMPLE = DEC_BATCH * DEC_SEQ
N_TOK = N_PROMPT + N_SAMPLE
WIDTH_A = N_HEADS_A * HEAD_DIM
WIDTH_KVA = 2 * N_KV_A * HEAD_DIM
WIDTH_B = N_HEADS_B * HEAD_DIM
LANES = 128
ROW_CHUNKS = D_MODEL // LANES
MOE_ROWS = 256
N_ASSIGN = TOP_K * N_TOK
N_MOE_BLOCKS = N_ASSIGN // MOE_ROWS + N_EXPERTS
N_MOE_ROWS = N_MOE_BLOCKS * MOE_ROWS
VMEM_LIMIT = 56 * 1024 * 1024

TM = 768
CAST_ROWS = 256


def _params(sem):
    return pltpu.CompilerParams(dimension_semantics=sem, vmem_limit_bytes=VMEM_LIMIT)


def _cast_weight(w_ref, wbf_ref):
    steps = w_ref.shape[0] // CAST_ROWS

    def body(i, c):
        rows = pl.ds(pl.multiple_of(i * CAST_ROWS, CAST_ROWS), CAST_ROWS)
        wbf_ref[rows, :] = w_ref[rows, :].astype(BF16)
        return c

    lax.fori_loop(0, steps, body, 0)


def _rmsnorm_rows(x, g):
    ms = jnp.mean(x * x, axis=-1, keepdims=True)
    return x * lax.rsqrt(ms + RMS_EPS) * g


def _norm_kernel(x_ref, g_ref, o_ref):
    o_ref[...] = _rmsnorm_rows(x_ref[...], g_ref[...]).astype(o_ref.dtype)


def _norm_call(x, g, out_dtype, tm=256):
    rows, width = x.shape
    return pl.pallas_call(
        _norm_kernel,
        grid=(rows // tm,),
        in_specs=[pl.BlockSpec((tm, width), lambda i: (i, 0)),
                  pl.BlockSpec((1, width), lambda i: (0, 0))],
        out_specs=pl.BlockSpec((tm, width), lambda i: (i, 0)),
        out_shape=jax.ShapeDtypeStruct((rows, width), out_dtype),
        compiler_params=_params(("arbitrary",)),
        name="rmsnorm",
    )(x, g.reshape(1, width))


QKV_WIDTHS = (WIDTH_A, WIDTH_KVA, WIDTH_B, WIDTH_B, WIDTH_B)
QKV_DIM = sum(QKV_WIDTHS)
TM_QKV = 384
TN_QKV = 512


def _qkv_kernel(x_ref, g_ref, w_ref, *o_refs):
    xn = _rmsnorm_rows(x_ref[...], g_ref[...]).astype(BF16)
    col = 0
    for o_ref in o_refs:
        width = o_ref.shape[1]
        for c in range(0, width, TN_QKV):
            cols = min(TN_QKV, width - c)
            o_ref[:, c:c + cols] = jnp.dot(xn, w_ref[:, col + c:col + c + cols], preferred_element_type=F32)
        col += width


def _to_bf16_kernel(w_ref, o_ref):
    o_ref[...] = w_ref[...].astype(BF16)


def _to_bf16_call(w, layer, rows=256):
    k, n = w.shape[1:]
    return pl.pallas_call(
        _to_bf16_kernel,
        grid=(k // rows,),
        in_specs=[pl.BlockSpec((None, rows, n), lambda i: (layer, i, 0))],
        out_specs=pl.BlockSpec((rows, n), lambda i: (i, 0)),
        out_shape=jax.ShapeDtypeStruct((k, n), BF16),
        compiler_params=_params(("arbitrary",)),
        name="weight_to_bf16",
    )(w)


def _qkv_call(x, g, w_in, layer):
    w_bf = _to_bf16_call(w_in, layer)
    return pl.pallas_call(
        _qkv_kernel,
        grid=(N_TOK // TM_QKV,),
        in_specs=[pl.BlockSpec((TM_QKV, D_MODEL), lambda m: (m, 0)),
                  pl.BlockSpec((1, D_MODEL), lambda m: (0, 0)),
                  pl.BlockSpec((D_MODEL, QKV_DIM), lambda m: (0, 0), pipeline_mode=pl.Buffered(1))],
        out_specs=[pl.BlockSpec((TM_QKV, w), lambda m: (m, 0)) for w in QKV_WIDTHS],
        out_shape=[jax.ShapeDtypeStruct((N_TOK, w), F32) for w in QKV_WIDTHS],
        compiler_params=_params(("arbitrary",)),
        name="qkv_proj",
    )(x, g.reshape(1, D_MODEL), w_bf)


def _mm_res_kernel(a_ref, w_ref, r_ref, o_ref, wbf_ref):
    @pl.when(pl.program_id(1) == 0)
    def _():
        _cast_weight(w_ref, wbf_ref)

    o_ref[...] = r_ref[...] + jnp.dot(a_ref[...], wbf_ref[...], preferred_element_type=F32)


def _mm_res_call(a, w, layer, res, tm, tn, name):
    m_rows, k = a.shape
    n_cols = w.shape[2]
    return pl.pallas_call(
        _mm_res_kernel,
        grid=(n_cols // tn, m_rows // tm),
        in_specs=[pl.BlockSpec((tm, k), lambda n, m: (m, 0)),
                  pl.BlockSpec((None, k, tn), lambda n, m: (layer, 0, n)),
                  pl.BlockSpec((tm, tn), lambda n, m: (m, n))],
        out_specs=pl.BlockSpec((tm, tn), lambda n, m: (m, n)),
        out_shape=jax.ShapeDtypeStruct((m_rows, n_cols), F32),
        scratch_shapes=[pltpu.VMEM((k, tn), BF16)],
        compiler_params=_params(("arbitrary", "arbitrary")),
        name=name,
    )(a, w, res)


def _silu_mul(g, u):
    return g * (1.0 / (1.0 + jnp.exp(-g))) * u


def _swiglu_kernel(a_ref, wg_ref, wu_ref, o_ref, wgbf_ref, wubf_ref):
    @pl.when(pl.program_id(1) == 0)
    def _():
        _cast_weight(wg_ref, wgbf_ref)
        _cast_weight(wu_ref, wubf_ref)

    a = a_ref[...]
    g = jnp.dot(a, wgbf_ref[...], preferred_element_type=F32)
    u = jnp.dot(a, wubf_ref[...], preferred_element_type=F32)
    o_ref[...] = _silu_mul(g, u).astype(o_ref.dtype)


def _swiglu_call(a, wg, wu, idx, tm=TM, tn=512):
    m_rows, k = a.shape
    n_cols = wg.shape[2]
    wspec = pl.BlockSpec((None, k, tn), lambda n, m: (idx, 0, n))
    return pl.pallas_call(
        _swiglu_kernel,
        grid=(n_cols // tn, m_rows // tm),
        in_specs=[pl.BlockSpec((tm, k), lambda n, m: (m, 0)), wspec, wspec],
        out_specs=pl.BlockSpec((tm, tn), lambda n, m: (m, n)),
        out_shape=jax.ShapeDtypeStruct((m_rows, n_cols), BF16),
        scratch_shapes=[pltpu.VMEM((k, tn), BF16), pltpu.VMEM((k, tn), BF16)],
        compiler_params=_params(("arbitrary", "arbitrary")),
        name="dense_gate_up",
    )(a, wg, wu)


def _t5_bucket_idx(dist):
    n = jnp.maximum(dist, 0)
    max_exact = NUM_BUCKETS // 2
    ratio = jnp.log(jnp.maximum(n, max_exact).astype(F32) / max_exact) / math.log(MAX_DISTANCE / max_exact)
    large = max_exact + (ratio * (NUM_BUCKETS - max_exact)).astype(jnp.int32)
    large = jnp.minimum(large, NUM_BUCKETS - 1)
    return jnp.where(n < max_exact, n, large)


def _bias_kernel(idx_ref, tab_ref, o_ref):
    idx = idx_ref[...]
    acc = jnp.full(idx.shape, NEG_INF, F32)
    for b in range(NUM_BUCKETS):
        acc = jnp.where(idx == b, tab_ref[b:b + 1, :], acc)
    o_ref[...] = acc


def _bias_call(idx, tab_rows):
    groups = tab_rows.shape[0]
    rows, cols = idx.shape
    return pl.pallas_call(
        _bias_kernel,
        grid=(groups,),
        in_specs=[pl.BlockSpec((rows, cols), lambda g: (0, 0)),
                  pl.BlockSpec((None, NUM_BUCKETS, cols), lambda g: (g, 0, 0))],
        out_specs=pl.BlockSpec((None, rows, cols), lambda g: (g, 0, 0)),
        out_shape=jax.ShapeDtypeStruct((groups, rows, cols), F32),
        compiler_params=_params(("arbitrary",)),
        name="rel_bias",
    )(idx, tab_rows)


def _band_bucket_idx(step, max_dist):
    i = lax.broadcasted_iota(jnp.int32, (BLOCK, 2 * BLOCK), 0)
    c = lax.broadcasted_iota(jnp.int32, (BLOCK, 2 * BLOCK), 1)
    dist = (i - c + BLOCK) * step
    valid = (dist >= 0) & (dist <= max_dist)
    return jnp.where(valid, _t5_bucket_idx(dist), -1)


def _dilation_multiplicity(dist):
    mult = jnp.zeros(dist.shape, jnp.int32)
    for w, r in DILATED:
        mult = mult + ((dist >= 0) & (dist <= w) & (dist % r == 0)).astype(jnp.int32)
    return mult


def _lane_half_mask():
    return lax.broadcasted_iota(jnp.int32, (BLOCK, LANES), 1) < HEAD_DIM


def _dup_head(t, kv):
    rolled = pltpu.roll(t, HEAD_DIM, axis=1)
    low = lax.broadcasted_iota(jnp.int32, t.shape, 1) < HEAD_DIM
    return jnp.where(low, t, rolled) if kv == 0 else jnp.where(low, rolled, t)


def _nt_dot(a, b):
    return lax.dot_general(a, b, (((1,), (1,)), ((), ())), preferred_element_type=F32)


def _band_scores(q2, kp, kc, bias, heads, first):
    low = _lane_half_mask()
    out = []
    for half, h in enumerate(heads):
        qh = jnp.where(low if half == 0 else ~low, q2, 0.0).astype(BF16)
        bias_prev = bias[h, :, 0:BLOCK]
        if first is not False:
            bias_prev = jnp.where(first, NEG_INF, bias_prev)
        out.append((_nt_dot(qh, kp) + bias_prev, _nt_dot(qh, kc) + bias[h, :, BLOCK:2 * BLOCK]))
    return out


def _band_weights(scores, heads, sink_ref):
    out = []
    for (sp, sc), h in zip(scores, heads):
        m = jnp.max(jnp.maximum(sp, sc), axis=1, keepdims=True)
        if sink_ref is not None:
            m = jnp.maximum(m, sink_ref[h])
        pp = jnp.exp(sp - m)
        pc = jnp.exp(sc - m)
        l = jnp.sum(pp + pc, axis=1, keepdims=True)
        if sink_ref is not None:
            l = l + jnp.exp(sink_ref[h] - m)
        out.append((m, l, pp.astype(BF16), pc.astype(BF16)))
    return out


def _band_values(weights, vp, vc):
    low = _lane_half_mask()
    os_ = [jnp.dot(pp, vp, preferred_element_type=F32) + jnp.dot(pc, vc, preferred_element_type=F32)
           for _, _, pp, pc in weights]
    pick = lambda a, b: jnp.where(low, a, b)
    return pick(weights[0][0], weights[1][0]), pick(weights[0][1], weights[1][1]), pick(os_[0], os_[1])


SCORE_LOOKAHEAD = 2


def _pipelined_pairs(n_items, scores_of, values_of, finish, sink_ref=None):
    pending = {}
    for i in range(n_items + SCORE_LOOKAHEAD):
        if i < n_items:
            q2, kp, kc, bias, heads, first = scores_of(i)
            pending[i] = (heads, _band_scores(q2, kp, kc, bias, heads, first))
        j = i - SCORE_LOOKAHEAD
        if j >= 0:
            heads, scores = pending.pop(j)
            finish(j, *_band_values(_band_weights(scores, heads, sink_ref), *values_of(j)))


def _swa_kernel(sink_ref, q_ref, kvp_ref, kvc_ref, bm_ref, o_ref):
    first = pl.program_id(1) == 0
    dup = lambda ref, cols: [_dup_head(ref[:, cols], kv).astype(BF16) for kv in range(N_KV_A)]
    kprev, kcur = dup(kvp_ref, slice(0, LANES)), dup(kvc_ref, slice(0, LANES))
    vprev, vcur = dup(kvp_ref, slice(LANES, 2 * LANES)), dup(kvc_ref, slice(LANES, 2 * LANES))
    cols = lambda p: slice(p * LANES, (p + 1) * LANES)
    kv_of = lambda p: (2 * p) // (N_HEADS_A // N_KV_A)

    def scores_of(p):
        return q_ref[:, cols(p)] * ATTN_SCALE, kprev[kv_of(p)], kcur[kv_of(p)], bm_ref, (2 * p, 2 * p + 1), first

    def finish(p, m, l, o):
        o_ref[:, cols(p)] = o / l

    _pipelined_pairs(WIDTH_A // LANES, scores_of, lambda p: (vprev[kv_of(p)], vcur[kv_of(p)]), finish, sink_ref)


def _swa_prompt_call(qa, kva, bm, sink):
    nb = SEQ // BLOCK
    cur = lambda b, n: (b * nb + n, 0)
    prev = lambda b, n: (b * nb + jnp.maximum(n - 1, 0), 0)
    return pl.pallas_call(
        _swa_kernel,
        grid=(BATCH, nb),
        in_specs=[pl.BlockSpec(memory_space=pltpu.SMEM),
                  pl.BlockSpec((BLOCK, WIDTH_A), cur),
                  pl.BlockSpec((BLOCK, WIDTH_KVA), prev),
                  pl.BlockSpec((BLOCK, WIDTH_KVA), cur),
                  pl.BlockSpec((N_HEADS_A, BLOCK, 2 * BLOCK), lambda b, n: (0, 0, 0))],
        out_specs=pl.BlockSpec((BLOCK, WIDTH_A), cur),
        out_shape=jax.ShapeDtypeStruct((N_PROMPT, WIDTH_A), F32),
        compiler_params=_params(("arbitrary", "arbitrary")),
        name="swa_prompt",
    )(sink, qa, kva, kva, bm)


SUPER = BLOCK * DILATED[-1][1]
HEADS_PER_STEP = 2
STEP_WIDTH = HEADS_PER_STEP * HEAD_DIM
BLOCKS_PER_STEP = 4
SUBLANES = 8


def _dilated_kernel(q_ref, k_ref, v_ref, bm_ref, o_ref, kwin, vwin, m_s, l_s, acc_s):
    n = pl.program_id(2)

    @pl.when(n == 0)
    def _():
        kwin[0:SUPER, :] = jnp.zeros((SUPER, STEP_WIDTH), F32)
        vwin[0:SUPER, :] = jnp.zeros((SUPER, STEP_WIDTH), F32)

    @pl.when(n > 0)
    def _():
        kwin[0:SUPER, :] = kwin[SUPER:2 * SUPER, :]
        vwin[0:SUPER, :] = vwin[SUPER:2 * SUPER, :]

    kwin[SUPER:2 * SUPER, :] = k_ref[...]
    vwin[SUPER:2 * SUPER, :] = v_ref[...]

    def sweep(d, r, starts, firsts):
        last = d == len(DILATED) - 1
        rows = lambda s: pl.ds(s, BLOCK, stride=r) if r > 1 else pl.ds(s, BLOCK)

        def scores_of(i):
            q2 = q_ref[rows(starts[i]), :] * ATTN_SCALE
            kp = kwin[rows(SUPER + starts[i] - BLOCK * r), :].astype(BF16)
            kc = kwin[rows(SUPER + starts[i]), :].astype(BF16)
            return q2, kp, kc, bm_ref.at[d], (0, 1), firsts[i]

        def values_of(i):
            return (vwin[rows(SUPER + starts[i] - BLOCK * r), :].astype(BF16),
                    vwin[rows(SUPER + starts[i]), :].astype(BF16))

        def finish(i, m, l, o):
            at = rows(starts[i])
            if d > 0:
                m_old = m_s[at, :]
                m_new = jnp.maximum(m_old, m)
                a_old = jnp.exp(m_old - m_new)
                a_new = jnp.exp(m - m_new)
                l = a_old * l_s[at, :] + a_new * l
                o = a_old * acc_s[at, :] + a_new * o
                m = m_new
            if last:
                o_ref[at, :] = o / l
            else:
                m_s[at, :] = m
                l_s[at, :] = l
                acc_s[at, :] = o

        _pipelined_pairs(len(starts), scores_of, values_of, finish)

    for d, (_, r) in enumerate(DILATED):
        span = BLOCK * r
        per_step = max(BLOCKS_PER_STEP, min(r, SUBLANES))
        if r >= per_step:
            groups = r // per_step
            assert groups == 1 or per_step % SUBLANES == 0
            offsets = tuple(range(per_step))
            base_of = lambda i, groups=groups, span=span, per_step=per_step: (
                pl.multiple_of((i // groups) * span + (i % groups) * per_step, SUBLANES))
            firsts_of = lambda i, groups=groups, per_step=per_step: [(n == 0) & (i // groups == 0)] * per_step
            steps = (SUPER // span) * groups
        else:
            offsets = tuple(k * span for k in range(per_step))
            base_of = lambda i, span=span, per_step=per_step: pl.multiple_of(i * span * per_step, SUBLANES)
            firsts_of = lambda i, per_step=per_step: [(n == 0) & (i == 0)] + [False] * (per_step - 1)
            steps = SUPER // (span * per_step)

        def body(i, c, d=d, r=r, offsets=offsets, base_of=base_of, firsts_of=firsts_of):
            base = base_of(i)
            sweep(d, r, [base + off for off in offsets], firsts_of(i))
            return c

        lax.fori_loop(0, steps, body, 0)


def _dilated_prompt_call(qb, kb, vb, bm):
    tiles = SEQ // SUPER
    tile = pl.BlockSpec((SUPER, STEP_WIDTH), lambda b, g, n: (b * tiles + n, g))
    return pl.pallas_call(
        _dilated_kernel,
        grid=(BATCH, N_HEADS_B // HEADS_PER_STEP, tiles),
        in_specs=[tile, tile, tile,
                  pl.BlockSpec((len(DILATED), HEADS_PER_STEP, BLOCK, 2 * BLOCK), lambda b, g, n: (0, g, 0, 0))],
        out_specs=tile,
        out_shape=jax.ShapeDtypeStruct((N_PROMPT, WIDTH_B), F32),
        scratch_shapes=[pltpu.VMEM((2 * SUPER, STEP_WIDTH), F32), pltpu.VMEM((2 * SUPER, STEP_WIDTH), F32),
                        pltpu.VMEM((SUPER, STEP_WIDTH), F32), pltpu.VMEM((SUPER, STEP_WIDTH), F32),
                        pltpu.VMEM((SUPER, STEP_WIDTH), F32)],
        compiler_params=_params(("arbitrary", "arbitrary", "arbitrary")),
        name="dilated_prompt",
    )(qb, kb, vb, bm)


def _merge_kernel(oa_ref, ob_ref, ga_ref, gb_ref, o_ref):
    o_ref[:, 0:WIDTH_A] = _rmsnorm_rows(oa_ref[...], ga_ref[...]).astype(o_ref.dtype)
    o_ref[:, WIDTH_A:] = _rmsnorm_rows(ob_ref[...], gb_ref[...]).astype(o_ref.dtype)


def _merge_call(oa, ob, ga, gb, tm=256):
    row = lambda w: pl.BlockSpec((tm, w), lambda i: (i, 0))
    vec = lambda w: pl.BlockSpec((1, w), lambda i: (0, 0))
    return pl.pallas_call(
        _merge_kernel,
        grid=(N_PROMPT // tm,),
        in_specs=[row(WIDTH_A), row(WIDTH_B), vec(WIDTH_A), vec(WIDTH_B)],
        out_specs=row(WIDTH_A + WIDTH_B),
        out_shape=jax.ShapeDtypeStruct((N_PROMPT, WIDTH_A + WIDTH_B), BF16),
        compiler_params=_params(("arbitrary",)),
        name="merge_prompt",
    )(oa, ob, ga.reshape(1, -1), gb.reshape(1, -1))


NEW_ROWS = 128
SINK_ROW = DEC_SEQ
HEAD_GROUPS = 2
GROUP_WIDTH = WIDTH_B // HEAD_GROUPS


def _pad_rows(t, rows):
    return jnp.concatenate([t, jnp.zeros((rows - t.shape[0], t.shape[1]), t.dtype)], axis=0)


def _sample_window_a(qa_ref, kva_ref, cak_ref, cav_ref, bac_ref, ban_ref, sink_ref):
    group = N_HEADS_A // N_KV_A
    lane8 = lax.broadcasted_iota(jnp.int32, (DEC_SEQ, LANES), 1) < HEAD_DIM
    rows = []
    for p in range(WIDTH_A // LANES):
        blk = qa_ref[:, p * LANES:(p + 1) * LANES] * ATTN_SCALE
        rolled = pltpu.roll(blk, HEAD_DIM, axis=1)
        if (2 * p) // group == 0:
            rows += [jnp.where(lane8, blk, 0.0), jnp.where(lane8, rolled, 0.0)]
        else:
            rows += [jnp.where(lane8, 0.0, rolled), jnp.where(lane8, 0.0, blk)]
    qat = jnp.concatenate(rows, axis=0).astype(BF16)
    kan = _pad_rows(kva_ref[:, 0:LANES], NEW_ROWS)
    van = _pad_rows(kva_ref[:, LANES:2 * LANES], NEW_ROWS)
    row_id = lax.broadcasted_iota(jnp.int32, (NEW_ROWS, LANES), 0)
    sc = _nt_dot(cak_ref[...].astype(BF16), qat) + bac_ref[...]
    sn = _nt_dot(kan.astype(BF16), qat) + jnp.where(row_id == SINK_ROW, sink_ref[...], ban_ref[...])
    m = jnp.maximum(jnp.max(sc, axis=0, keepdims=True), jnp.max(sn, axis=0, keepdims=True))
    pct = jnp.exp(sc - m).T
    pnt = jnp.exp(sn - m).T
    l_col = jnp.sum(pct, axis=1, keepdims=True) + jnp.sum(pnt, axis=1, keepdims=True)
    pct = pct.astype(BF16)
    pnt = pnt.astype(BF16)
    res = []
    for kv in range(N_KV_A):
        vc = _dup_head(cav_ref[...], kv).astype(BF16)
        vn = _dup_head(van, kv).astype(BF16)
        res.append((jnp.dot(pct, vc, preferred_element_type=F32)
                    + jnp.dot(pnt, vn, preferred_element_type=F32)) / l_col)
    pairs = []
    for p in range(WIDTH_A // LANES):
        r = res[(2 * p) // group]
        pairs.append(jnp.where(lane8, r[16 * p:16 * p + 8, :], r[16 * p + 8:16 * p + 16, :]))
    return jnp.concatenate(pairs, axis=1)


KEEP_LANES = LANES - DEC_SEQ


def _shift_lanes(old_ref, out_ref, tail):
    feats, window = old_ref.shape
    lane = lax.broadcasted_iota(jnp.int32, (feats, LANES), 1)
    nxt = tail
    for j in reversed(range(window // LANES)):
        cur = pltpu.roll(old_ref[:, j * LANES:(j + 1) * LANES], KEEP_LANES, axis=1)
        out_ref[:, j * LANES:(j + 1) * LANES] = jnp.where(lane < KEEP_LANES, cur, nxt)
        nxt = cur


def _shift_in_new(old_ref, new_ref, out_ref):
    tails = []
    for c in range(old_ref.shape[0] // LANES):
        sq = _pad_rows(new_ref[:, c * LANES:(c + 1) * LANES], LANES).T
        tails.append(pltpu.roll(sq, KEEP_LANES, axis=1))
    _shift_lanes(old_ref, out_ref, jnp.concatenate(tails, axis=0))


def _sample_kernel(qa_ref, kva_ref, qb_ref, kb_ref, vb_ref, cak_ref, cav_ref, cbk_ref, cbv_ref,
                   bac_ref, ban_ref, sink_ref, bbc_ref, bbn_ref, mbc_ref, mbn_ref, ga_ref, gb_ref,
                   o_ref, oa_s, ob_s):
    g = pl.program_id(1)

    @pl.when(g == 0)
    def _():
        oa_s[...] = _sample_window_a(qa_ref, kva_ref, cak_ref, cav_ref, bac_ref, ban_ref, sink_ref)

    low = lax.broadcasted_iota(jnp.int32, (DEC_SEQ, LANES), 1) < HEAD_DIM
    n_pairs = GROUP_WIDTH // LANES
    cols = lambda p: slice(p * LANES, (p + 1) * LANES)

    def scores(p):
        q2 = qb_ref[:, cols(p)] * ATTN_SCALE
        kt = cbk_ref[cols(p), :].astype(BF16)
        kn = _pad_rows(kb_ref[:, cols(p)], NEW_ROWS).astype(BF16)
        out = []
        for half in range(2):
            qh = jnp.where(low if half == 0 else ~low, q2, 0.0).astype(BF16)
            out.append((jnp.dot(qh, kt, preferred_element_type=F32) + bbc_ref[2 * p + half],
                        _nt_dot(qh, kn) + bbn_ref[2 * p + half]))
        return out

    def values(p, pair_scores):
        vt = cbv_ref[cols(p), :].astype(BF16)
        vn = _pad_rows(vb_ref[:, cols(p)], NEW_ROWS).astype(BF16)
        halves = []
        for sc, sn in pair_scores:
            m = jnp.maximum(jnp.max(sc, axis=1, keepdims=True), jnp.max(sn, axis=1, keepdims=True))
            pc = jnp.exp(sc - m) * mbc_ref[...]
            pn = jnp.exp(sn - m) * mbn_ref[...]
            l = jnp.sum(pc, axis=1, keepdims=True) + jnp.sum(pn, axis=1, keepdims=True)
            o = _nt_dot(pc.astype(BF16), vt) + jnp.dot(pn.astype(BF16), vn, preferred_element_type=F32)
            halves.append(o / l)
        return jnp.where(low, halves[0], halves[1])

    pending, pairs = {}, []
    for p in range(n_pairs + SCORE_LOOKAHEAD):
        if p < n_pairs:
            pending[p] = scores(p)
        if p >= SCORE_LOOKAHEAD:
            pairs.append(values(p - SCORE_LOOKAHEAD, pending.pop(p - SCORE_LOOKAHEAD)))
    ob = jnp.concatenate(pairs, axis=1)
    for k in range(HEAD_GROUPS):
        @pl.when(g == k)
        def _(k=k):
            ob_s[:, k * GROUP_WIDTH:(k + 1) * GROUP_WIDTH] = ob

    @pl.when(g == HEAD_GROUPS - 1)
    def _():
        o_ref[:, 0:WIDTH_A] = _rmsnorm_rows(oa_s[...], ga_ref[...])
        o_ref[:, WIDTH_A:] = _rmsnorm_rows(ob_s[...], gb_ref[...])


def _sample_call(qa, kva, qb, kb, vb, cak, cav, cbk_t, cbv_t, consts, sink_row, ga, gb, layer):
    first = N_PROMPT // DEC_SEQ
    new = lambda w: pl.BlockSpec((DEC_SEQ, w), lambda b, g: (first + b, 0))
    new_g = pl.BlockSpec((DEC_SEQ, GROUP_WIDTH), lambda b, g: (first + b, g))
    la, lb = cak.shape[2], cbk_t.shape[3]
    cache_a = pl.BlockSpec((None, None, la, LANES), lambda b, g: (layer, b, 0, 0))
    cache_b = pl.BlockSpec((None, None, GROUP_WIDTH, lb), lambda b, g: (layer, b, g, 0))
    const = lambda *shape: pl.BlockSpec(shape, lambda b, g: (0,) * len(shape))
    heads = N_HEADS_B // HEAD_GROUPS
    per_head = lambda *shape: pl.BlockSpec((heads,) + shape, lambda b, g: (g,) + (0,) * len(shape))
    in_specs = [new(WIDTH_A), new(WIDTH_KVA), new_g, new_g, new_g, cache_a, cache_a, cache_b, cache_b,
                const(la, LANES), const(NEW_ROWS, LANES), const(1, LANES),
                per_head(DEC_SEQ, lb), per_head(DEC_SEQ, NEW_ROWS), const(DEC_SEQ, lb), const(DEC_SEQ, NEW_ROWS),
                const(1, WIDTH_A), const(1, WIDTH_B)]
    args = [qa, kva, qb, kb, vb, cak, cav, cbk_t, cbv_t,
            consts["bac"], consts["ban"], sink_row, consts["bbc"], consts["bbn"], consts["mbc"], consts["mbn"],
            ga.reshape(1, -1), gb.reshape(1, -1)]
    return pl.pallas_call(
        _sample_kernel,
        grid=(DEC_BATCH, HEAD_GROUPS),
        in_specs=in_specs,
        out_specs=pl.BlockSpec((DEC_SEQ, WIDTH_A + WIDTH_B), lambda b, g: (b, 0)),
        out_shape=jax.ShapeDtypeStruct((N_SAMPLE, WIDTH_A + WIDTH_B), F32),
        scratch_shapes=[pltpu.VMEM((DEC_SEQ, WIDTH_A), F32), pltpu.VMEM((DEC_SEQ, WIDTH_B), F32)],
        compiler_params=_params(("arbitrary", "arbitrary")),
        name="sample_attention",
    )(*args)


def _cache_tail_kernel(old_ref, new_ref, shifted_hbm, out_ref):
    del shifted_hbm
    _shift_in_new(old_ref, new_ref, out_ref)


def _cache_tail_call(cache_t, shifted, new_rows):
    last_tile = cache_t.shape[3] // LANES - 1
    tile = pl.BlockSpec((None, None, GROUP_WIDTH, LANES), lambda l, b, g: (l, b, g, last_tile))
    return pl.pallas_call(
        _cache_tail_kernel,
        grid=(DEPTH, DEC_BATCH, HEAD_GROUPS),
        in_specs=[tile, pl.BlockSpec((None, DEC_SEQ, GROUP_WIDTH), lambda l, b, g: (l, b, g)),
                  pl.BlockSpec(memory_space=pl.ANY)],
        out_specs=tile,
        out_shape=jax.ShapeDtypeStruct(cache_t.shape, F32),
        input_output_aliases={2: 0},
        compiler_params=_params(("arbitrary", "arbitrary", "arbitrary")),
        name="cache_tail",
    )(cache_t, new_rows, shifted)


def _sample_consts(rel_bias):
    la, lb = WINDOW_A, WINDOW_B

    def idx_of(dist, valid):
        return jnp.where(valid, _t5_bucket_idx(dist), -1)

    t = lax.broadcasted_iota(jnp.int32, (1, LANES), 1) % DEC_SEQ
    tab_a = jnp.repeat(rel_bias[:, :N_HEADS_A], DEC_SEQ, axis=1)[None]
    i_a = lax.broadcasted_iota(jnp.int32, (la, LANES), 0)
    dist = la + t - i_a
    bac = _bias_call(idx_of(dist, dist <= WINDOW_A - 1), tab_a)[0]
    j = lax.broadcasted_iota(jnp.int32, (NEW_ROWS, LANES), 0)
    dist_n = t - j
    ban = _bias_call(idx_of(dist_n, (dist_n >= 0) & (j < DEC_SEQ)), tab_a)[0]
    tab_b = rel_bias[:, N_HEADS_A:].T[:, :, None]
    tq = lax.broadcasted_iota(jnp.int32, (DEC_SEQ, lb), 0)
    ik = lax.broadcasted_iota(jnp.int32, (DEC_SEQ, lb), 1)
    dist_b = lb + tq - ik
    mult_c = _dilation_multiplicity(dist_b)
    bbc = _bias_call(idx_of(dist_b, mult_c > 0), jnp.broadcast_to(tab_b, (N_HEADS_B, NUM_BUCKETS, lb)))
    tn = lax.broadcasted_iota(jnp.int32, (DEC_SEQ, NEW_ROWS), 0)
    jn = lax.broadcasted_iota(jnp.int32, (DEC_SEQ, NEW_ROWS), 1)
    mult_n = jnp.where(jn < DEC_SEQ, _dilation_multiplicity(tn - jn), 0)
    bbn = _bias_call(idx_of(tn - jn, mult_n > 0), jnp.broadcast_to(tab_b, (N_HEADS_B, NUM_BUCKETS, NEW_ROWS)))
    return {"bac": bac, "ban": ban, "bbc": bbc, "bbn": bbn,
            "mbc": mult_c.astype(F32), "mbn": mult_n.astype(F32)}


def _router_kernel(x_ref, rt_ref, idx_ref, gate_ref):
    logits = lax.dot_general(rt_ref[...], x_ref[...], (((1,), (1,)), ((), ())),
                             precision=lax.Precision.HIGHEST, preferred_element_type=F32)
    e_id = lax.broadcasted_iota(jnp.int32, logits.shape, 0)
    m1 = jnp.max(logits, axis=0, keepdims=True)
    i1 = jnp.min(jnp.where(logits == m1, e_id, N_EXPERTS), axis=0, keepdims=True)
    rest = jnp.where(e_id == i1, -jnp.inf, logits)
    m2 = jnp.max(rest, axis=0, keepdims=True)
    i2 = jnp.min(jnp.where(rest == m2, e_id, N_EXPERTS), axis=0, keepdims=True)
    e2 = jnp.exp(m2 - m1)
    den = 1.0 + e2
    idx_ref[0:1, :] = i1
    idx_ref[1:2, :] = i2
    gate_ref[0:1, :] = 1.0 / den
    gate_ref[1:2, :] = e2 / den


def _router_call(xn, router_t, tm=TM):
    return pl.pallas_call(
        _router_kernel,
        grid=(N_TOK // tm,),
        in_specs=[pl.BlockSpec((tm, D_MODEL), lambda i: (i, 0)),
                  pl.BlockSpec((N_EXPERTS, D_MODEL), lambda i: (0, 0))],
        out_specs=[pl.BlockSpec((TOP_K, tm), lambda i: (0, i)),
                   pl.BlockSpec((TOP_K, tm), lambda i: (0, i))],
        out_shape=[jax.ShapeDtypeStruct((TOP_K, N_TOK), jnp.int32),
                   jax.ShapeDtypeStruct((TOP_K, N_TOK), F32)],
        compiler_params=_params(("arbitrary",)),
        name="moe_router",
    )(xn, router_t)


def _routing_plan(top_i):
    e = top_i.T.reshape(-1)
    order = jnp.argsort(e, stable=True)
    onehot = (e[:, None] == jnp.arange(N_EXPERTS)[None, :]).astype(jnp.int32)
    counts = jnp.sum(onehot, axis=0)
    starts = jnp.cumsum(counts) - counts
    padded = (counts + MOE_ROWS - 1) // MOE_ROWS * MOE_ROWS
    pstarts = jnp.cumsum(padded) - padded
    blk_start = jnp.arange(N_MOE_BLOCKS) * MOE_ROWS
    blk_e = jnp.minimum(jnp.sum(blk_start[:, None] >= (pstarts + padded)[None, :], axis=1), N_EXPERTS - 1)
    n_used = (jnp.sum(padded) // MOE_ROWS).astype(jnp.int32).reshape(1)
    row = jnp.arange(N_MOE_ROWS)
    row_e = blk_e[row // MOE_ROWS]
    within = row - pstarts[row_e]
    src = order[jnp.clip(starts[row_e] + within, 0, N_ASSIGN - 1)]
    row_tok = jnp.where(within < counts[row_e], src // TOP_K, 0).astype(jnp.int32)
    rank = jnp.take_along_axis(jnp.cumsum(onehot, axis=0) - onehot, e[:, None], axis=1)[:, 0]
    dest = (pstarts[e] + rank).astype(jnp.int32)
    dest_kt = dest.reshape(N_TOK, TOP_K).T.reshape(-1)
    return blk_e.astype(jnp.int32), n_used, row_tok, dest_kt


def _row_copy(src_hbm, row, dst, slot, sem):
    return pltpu.make_async_copy(
        src_hbm.at[pl.ds(pl.multiple_of(row * ROW_CHUNKS, ROW_CHUNKS), ROW_CHUNKS), :],
        dst.at[pl.ds(pl.multiple_of(slot * ROW_CHUNKS, ROW_CHUNKS), ROW_CHUNKS), :],
        sem)


def _wait_rows(src_hbm, dst, sem):
    pltpu.make_async_copy(src_hbm.at[pl.ds(0, dst.shape[0]), :], dst, sem).wait()


ISSUE_UNROLL = 8


def _gather_kernel(tok_ref, x_hbm, o_ref, buf0, buf1, sems):
    b = pl.program_id(0)
    bufs = (buf0, buf1)

    def request(block, slot):
        def issue(i, c):
            for u in range(ISSUE_UNROLL):
                row = i * ISSUE_UNROLL + u
                _row_copy(x_hbm, tok_ref[block * MOE_ROWS + row], bufs[slot], row, sems.at[slot]).start()
            return c

        lax.fori_loop(0, MOE_ROWS // ISSUE_UNROLL, issue, 0)

    @pl.when(b == 0)
    def _():
        request(0, 0)

    for slot in range(2):
        @pl.when((b + 1 < N_MOE_BLOCKS) & ((b + 1) % 2 == slot))
        def _(slot=slot):
            request(b + 1, slot)

    for slot in range(2):
        @pl.when(b % 2 == slot)
        def _(slot=slot):
            _wait_rows(x_hbm, bufs[slot], sems.at[slot])
            for j in range(ROW_CHUNKS):
                chunk = bufs[slot][pl.ds(j, MOE_ROWS, stride=ROW_CHUNKS), :]
                o_ref[:, j * LANES:(j + 1) * LANES] = chunk.astype(o_ref.dtype)


def _gather_call(row_tok, x_chunks):
    grid_spec = pltpu.PrefetchScalarGridSpec(
        num_scalar_prefetch=1,
        grid=(N_MOE_BLOCKS,),
        in_specs=[pl.BlockSpec(memory_space=pl.ANY)],
        out_specs=pl.BlockSpec((MOE_ROWS, D_MODEL), lambda b, tok: (b, 0)),
        scratch_shapes=[pltpu.VMEM((MOE_ROWS * ROW_CHUNKS, LANES), F32),
                        pltpu.VMEM((MOE_ROWS * ROW_CHUNKS, LANES), F32),
                        pltpu.SemaphoreType.DMA((2,))],
    )
    return pl.pallas_call(
        _gather_kernel,
        grid_spec=grid_spec,
        out_shape=jax.ShapeDtypeStruct((N_MOE_ROWS, D_MODEL), BF16),
        compiler_params=_params(("arbitrary",)),
        name="moe_dispatch",
    )(row_tok, x_chunks)


def _expert_changed(be_ref, nu_ref):
    b = pl.program_id(1)
    last = jnp.minimum(b, nu_ref[0] - 1)
    return (b < nu_ref[0]) & ((b == 0) | (be_ref[last] != be_ref[jnp.maximum(last - 1, 0)]))


SHIFT_ROWS = 256
SHIFT_BLOCKS = DEPTH * DEC_BATCH * (WIDTH_B // SHIFT_ROWS)


def _moe_up_kernel(be_ref, nu_ref, a_ref, wg_ref, wu_ref, cache_ref, o_ref, shifted_ref, wgbf_ref, wubf_ref):
    @pl.when(_expert_changed(be_ref, nu_ref))
    def _():
        _cast_weight(wg_ref, wgbf_ref)
        _cast_weight(wu_ref, wubf_ref)

    used = pl.program_id(1) < nu_ref[0]
    shift = lambda: _shift_lanes(cache_ref, shifted_ref, jnp.zeros((SHIFT_ROWS, LANES), F32))

    @pl.when(used)
    def _():
        a = a_ref[...]
        g = jnp.dot(a, wgbf_ref[...], preferred_element_type=F32)
        u = jnp.dot(a, wubf_ref[...], preferred_element_type=F32)
        o_ref[...] = _silu_mul(g, u).astype(o_ref.dtype)
        shift()

    @pl.when(jnp.logical_not(used))
    def _():
        o_ref[...] = jnp.zeros(o_ref.shape, o_ref.dtype)
        shift()


def _moe_down_kernel(be_ref, nu_ref, a_ref, w_ref, o_ref, wbf_ref):
    @pl.when(_expert_changed(be_ref, nu_ref))
    def _():
        _cast_weight(w_ref, wbf_ref)

    used = pl.program_id(1) < nu_ref[0]

    @pl.when(used)
    def _():
        o_ref[...] = jnp.dot(a_ref[...], wbf_ref[...], preferred_element_type=F32)

    @pl.when(jnp.logical_not(used))
    def _():
        o_ref[...] = jnp.zeros(o_ref.shape, o_ref.dtype)


def _moe_specs(k, tn, idx):
    last = lambda b, nu: jnp.minimum(b, nu[0] - 1)
    a_spec = pl.BlockSpec((MOE_ROWS, k), lambda n, b, be, nu: (last(b, nu), 0))
    w_spec = pl.BlockSpec((None, None, k, tn), lambda n, b, be, nu: (idx, be[last(b, nu)], 0, n))
    o_spec = pl.BlockSpec((MOE_ROWS, tn), lambda n, b, be, nu: (b, n))
    return a_spec, w_spec, o_spec


def _moe_up_call(blk_e, n_used, a, wg, wu, idx, cache_t, tn=1024):
    k, n_cols = a.shape[1], wg.shape[3]
    a_spec, w_spec, o_spec = _moe_specs(k, tn, idx)
    assert (n_cols // tn) * N_MOE_BLOCKS >= SHIFT_BLOCKS
    per_entry = WIDTH_B // SHIFT_ROWS

    def cache_block(n, b, be, nu):
        s = jnp.minimum(n * N_MOE_BLOCKS + b, SHIFT_BLOCKS - 1)
        return (s // (DEC_BATCH * per_entry), (s // per_entry) % DEC_BATCH, s % per_entry, 0)

    cache_spec = pl.BlockSpec((None, None, SHIFT_ROWS, cache_t.shape[3]), cache_block)
    grid_spec = pltpu.PrefetchScalarGridSpec(
        num_scalar_prefetch=2,
        grid=(n_cols // tn, N_MOE_BLOCKS),
        in_specs=[a_spec, w_spec, w_spec, cache_spec],
        out_specs=[o_spec, cache_spec],
        scratch_shapes=[pltpu.VMEM((k, tn), BF16)] * 2,
    )
    return pl.pallas_call(
        _moe_up_kernel,
        grid_spec=grid_spec,
        out_shape=[jax.ShapeDtypeStruct((N_MOE_ROWS, n_cols), BF16), jax.ShapeDtypeStruct(cache_t.shape, F32)],
        compiler_params=_params(("arbitrary", "arbitrary")),
        name="moe_gate_up",
    )(blk_e, n_used, a, wg, wu, cache_t)


def _moe_down_call(blk_e, n_used, a, wd, idx, tn=512):
    k, n_cols = a.shape[1], wd.shape[3]
    a_spec, w_spec, o_spec = _moe_specs(k, tn, idx)
    grid_spec = pltpu.PrefetchScalarGridSpec(
        num_scalar_prefetch=2,
        grid=(n_cols // tn, N_MOE_BLOCKS),
        in_specs=[a_spec, w_spec],
        out_specs=o_spec,
        scratch_shapes=[pltpu.VMEM((k, tn), BF16)],
    )
    return pl.pallas_call(
        _moe_down_kernel,
        grid_spec=grid_spec,
        out_shape=jax.ShapeDtypeStruct((N_MOE_ROWS, n_cols), F32),
        compiler_params=_params(("arbitrary", "arbitrary")),
        name="moe_down",
    )(blk_e, n_used, a, wd)


COMBINE_ROWS = 256


def _combine_kernel(dest_ref, x_ref, g_ref, y_hbm, o_ref, buf0, buf1, sems):
    base = pl.program_id(0) * COMBINE_ROWS
    bufs = (buf0, buf1)

    def issue(i, c):
        for u in range(ISSUE_UNROLL):
            row = i * ISSUE_UNROLL + u
            for k in range(TOP_K):
                _row_copy(y_hbm, dest_ref[k * N_TOK + base + row], bufs[k], row, sems.at[k]).start()
        return c

    lax.fori_loop(0, COMBINE_ROWS // ISSUE_UNROLL, issue, 0)
    for k in range(TOP_K):
        _wait_rows(y_hbm, bufs[k], sems.at[k])
    g0 = g_ref[:, 0:1]
    g1 = g_ref[:, 1:2]
    for j in range(ROW_CHUNKS):
        cols = slice(j * LANES, (j + 1) * LANES)
        rows = pl.ds(j, COMBINE_ROWS, stride=ROW_CHUNKS)
        o_ref[:, cols] = x_ref[:, cols] + (g0 * buf0[rows, :] + g1 * buf1[rows, :])


def _combine_call(dest_kt, x, gates, y_chunks):
    grid_spec = pltpu.PrefetchScalarGridSpec(
        num_scalar_prefetch=1,
        grid=(N_TOK // COMBINE_ROWS,),
        in_specs=[pl.BlockSpec((COMBINE_ROWS, D_MODEL), lambda i, d: (i, 0)),
                  pl.BlockSpec((COMBINE_ROWS, TOP_K), lambda i, d: (i, 0)),
                  pl.BlockSpec(memory_space=pl.ANY)],
        out_specs=pl.BlockSpec((COMBINE_ROWS, D_MODEL), lambda i, d: (i, 0)),
        scratch_shapes=[pltpu.VMEM((COMBINE_ROWS * ROW_CHUNKS, LANES), F32),
                        pltpu.VMEM((COMBINE_ROWS * ROW_CHUNKS, LANES), F32),
                        pltpu.SemaphoreType.DMA((TOP_K,))],
    )
    return pl.pallas_call(
        _combine_kernel,
        grid_spec=grid_spec,
        out_shape=jax.ShapeDtypeStruct((N_TOK, D_MODEL), F32),
        compiler_params=_params(("arbitrary",)),
        name="moe_combine",
    )(dest_kt, x, gates, y_chunks)


def _moe_layer(x, g, router, wg, wu, wd, idx, cache_t):
    xn = _norm_call(x, g, F32)
    top_i, gates = _router_call(xn, router[idx].T)
    blk_e, n_used, row_tok, dest_kt = _routing_plan(top_i)
    xs = _gather_call(row_tok, xn.reshape(N_TOK * ROW_CHUNKS, LANES))
    h, shifted = _moe_up_call(blk_e, n_used, xs, wg, wu, idx, cache_t)
    y = _moe_down_call(blk_e, n_used, h, wd, idx)
    return _combine_call(dest_kt, x, gates.T, y.reshape(N_MOE_ROWS * ROW_CHUNKS, LANES)), shifted


def kernel(x_prompt, x_sample, cache_a_k, cache_a_v, cache_b_k, cache_b_v, g_mix_in, w_in, sinks, rel_bias,
           g_out_a, g_out_b, w_out, g_ffn, w_gate_d, w_up_d, w_down_d, router, w_gate_e, w_up_e, w_down_e,
           g_final):
    la, lb = cache_a_k.shape[2], cache_b_k.shape[2]
    x = jnp.concatenate([x_prompt.reshape(N_PROMPT, D_MODEL), x_sample.reshape(N_SAMPLE, D_MODEL)], axis=0)
    cak = cache_a_k.reshape(DEPTH, DEC_BATCH, la, N_KV_A * HEAD_DIM)
    cav = cache_a_v.reshape(DEPTH, DEC_BATCH, la, N_KV_A * HEAD_DIM)
    to_t = lambda c: jnp.transpose(c, (0, 1, 3, 4, 2)).reshape(DEPTH, DEC_BATCH, WIDTH_B, lb)
    from_t = lambda c: jnp.transpose(c.reshape(DEPTH, DEC_BATCH, N_HEADS_B, HEAD_DIM, lb), (0, 1, 4, 2, 3))
    cbk_t, cbv_t = to_t(cache_b_k), to_t(cache_b_v)

    head_rows = lambda tab: jnp.broadcast_to(tab.T[:, :, None], (tab.shape[1], NUM_BUCKETS, 2 * BLOCK))
    tab_a, tab_b = rel_bias[:, :N_HEADS_A], rel_bias[:, N_HEADS_A:]
    bm_a = _bias_call(_band_bucket_idx(1, WINDOW_A - 1), head_rows(tab_a))
    bm_b = jnp.stack([_bias_call(_band_bucket_idx(r, w), head_rows(tab_b)) for w, r in DILATED])
    consts = _sample_consts(rel_bias)

    tail = lambda t, keep: jnp.stack([t[(b + 1) * SEQ - keep:(b + 1) * SEQ] for b in range(BATCH)])
    kva_p, kb_p, vb_p, kva_s, kb_s, vb_s, shifted_caches = [], [], [], [], [], [], []
    for l in range(DEPTH):
        qa, kva, qb, kb, vb = _qkv_call(x, g_mix_in[l], w_in, l)
        oa = _swa_prompt_call(qa, kva, bm_a, sinks[l])
        ob = _dilated_prompt_call(qb, kb, vb, bm_b)
        y_prompt = _merge_call(oa, ob, g_out_a[l], g_out_b[l])
        sink_row = jnp.repeat(sinks[l], DEC_SEQ).reshape(1, LANES)
        y_sample = _sample_call(qa, kva, qb, kb, vb, cak, cav, cbk_t, cbv_t, consts, sink_row,
                                g_out_a[l], g_out_b[l], l)
        y = jnp.concatenate([y_prompt, y_sample.astype(BF16)], axis=0)
        x = _mm_res_call(y, w_out, l, x, TM, 512, "out_proj")
        if l % 2 == 0:
            hn = _norm_call(x, g_ffn[l], BF16)
            h = _swiglu_call(hn, w_gate_d, w_up_d, l // 2)
            x = _mm_res_call(h, w_down_d, l // 2, x, 384, 512, "dense_down")
        else:
            x, shifted = _moe_layer(x, g_ffn[l], router, w_gate_e, w_up_e, w_down_e, l // 2,
                                    (cbk_t, cbv_t)[(l // 2) % 2])
            shifted_caches.append(shifted)
        kva_p.append(tail(kva, min(WINDOW_A, SEQ)))
        kb_p.append(tail(kb, min(WINDOW_B, SEQ)))
        vb_p.append(tail(vb, min(WINDOW_B, SEQ)))
        kva_s.append(kva[N_PROMPT:])
        kb_s.append(kb[N_PROMPT:])
        vb_s.append(vb[N_PROMPT:])

    y = _norm_call(x, g_final, F32)
    y_prompt = y[:N_PROMPT].reshape(BATCH, SEQ, D_MODEL)
    y_sample = y[N_PROMPT:].reshape(DEC_BATCH, DEC_SEQ, D_MODEL)

    kva_p, kva_s = jnp.stack(kva_p), jnp.stack(kva_s)
    heads_a = lambda t: t.reshape(t.shape[:-1] + (N_KV_A, HEAD_DIM))
    heads_b = lambda t: t.reshape(t.shape[:-1] + (N_HEADS_B, HEAD_DIM))
    ka_p, va_p = heads_a(kva_p[..., :LANES]), heads_a(kva_p[..., LANES:])
    kb_p, vb_p = heads_b(jnp.stack(kb_p)), heads_b(jnp.stack(vb_p))
    ka_s = heads_a(kva_s[..., :LANES].reshape(DEPTH, DEC_BATCH, DEC_SEQ, LANES))
    va_s = heads_a(kva_s[..., LANES:].reshape(DEPTH, DEC_BATCH, DEC_SEQ, LANES))
    keep_a = min(WINDOW_A, la + DEC_SEQ)
    new_ak = jnp.concatenate([cache_a_k, ka_s], axis=2)[:, :, -keep_a:]
    new_av = jnp.concatenate([cache_a_v, va_s], axis=2)[:, :, -keep_a:]
    new_bk = from_t(_cache_tail_call(cbk_t, shifted_caches[0], jnp.stack(kb_s)))
    new_bv = from_t(_cache_tail_call(cbv_t, shifted_caches[1], jnp.stack(vb_s)))
    return (y_prompt, y_sample, ka_p, va_p, kb_p, vb_p, new_ak, new_av, new_bk, new_bv)
```

```python
import functools
import math

import jax
import jax.numpy as jnp
from jax import lax
from jax.experimental import pallas as pl
from jax.experimental.pallas import tpu as pltpu

F32 = jnp.float32
BF16 = jnp.bfloat16

D_MODEL = 2048
BATCH = 2
SEQ = 4096
DEPTH = 4
DEC_BATCH = 32
DEC_SEQ = 8
PAST_LEN = 16384
HEAD_DIM = 64
N_HEADS_A = 16
N_KV_A = 2
N_HEADS_B = 16
WINDOW_A = 128
DILATED = ((128, 1), (512, 4), (2048, 16))
WINDOW_B = 2048
BLOCK = 128
NUM_BUCKETS = 32
MAX_DISTANCE = 2048
ATTN_SCALE = HEAD_DIM ** -0.5
NEG_INF = -1e30
D_FF = 5632
N_EXPERTS = 8
TOP_K = 2
D_FF_EXPERT = 7168
RMS_EPS = 1e-5

N_PROMPT = BATCH * SEQ
N_SAMPLE = DEC_BATCH * DEC_SEQ
N_TOK = N_PROMPT + N_SAMPLE
WIDTH_A = N_HEADS_A * HEAD_DIM
WIDTH_KVA = 2 * N_KV_A * HEAD_DIM
WIDTH_B = N_HEADS_B * HEAD_DIM
LANES = 128
ROW_CHUNKS = D_MODEL // LANES
MOE_ROWS = 256
N_ASSIGN = TOP_K * N_TOK
N_MOE_BLOCKS = N_ASSIGN // MOE_ROWS + N_EXPERTS
N_MOE_ROWS = N_MOE_BLOCKS * MOE_ROWS
VMEM_LIMIT = 56 * 1024 * 1024

TM = 768
CAST_ROWS = 256


def _params(sem):
    return pltpu.CompilerParams(dimension_semantics=sem, vmem_limit_bytes=VMEM_LIMIT)


def _cast_weight(w_ref, wbf_ref):
    steps = w_ref.shape[0] // CAST_ROWS

    def body(i, c):
        rows = pl.ds(pl.multiple_of(i * CAST_ROWS, CAST_ROWS), CAST_ROWS)
        wbf_ref[rows, :] = w_ref[rows, :].astype(BF16)
        return c

    lax.fori_loop(0, steps, body, 0)


def _rmsnorm_rows(x, g):
    ms = jnp.mean(x * x, axis=-1, keepdims=True)
    return x * lax.rsqrt(ms + RMS_EPS) * g


def _norm_kernel(x_ref, g_ref, o_ref):
    o_ref[...] = _rmsnorm_rows(x_ref[...], g_ref[...]).astype(o_ref.dtype)


def _norm_call(x, g, out_dtype, tm=256):
    rows, width = x.shape
    return pl.pallas_call(
        _norm_kernel,
        grid=(rows // tm,),
        in_specs=[pl.BlockSpec((tm, width), lambda i: (i, 0)),
                  pl.BlockSpec((1, width), lambda i: (0, 0))],
        out_specs=pl.BlockSpec((tm, width), lambda i: (i, 0)),
        out_shape=jax.ShapeDtypeStruct((rows, width), out_dtype),
        compiler_params=_params(("arbitrary",)),
        name="rmsnorm",
    )(x, g.reshape(1, width))


QKV_WIDTHS = (WIDTH_A, WIDTH_KVA, WIDTH_B, WIDTH_B, WIDTH_B)
QKV_DIM = sum(QKV_WIDTHS)
TM_QKV = 384
TN_QKV = 512


def _qkv_kernel(x_ref, g_ref, w_ref, *o_refs):
    xn = _rmsnorm_rows(x_ref[...], g_ref[...]).astype(BF16)
    col = 0
    for o_ref in o_refs:
        width = o_ref.shape[1]
        for c in range(0, width, TN_QKV):
            cols = min(TN_QKV, width - c)
            o_ref[:, c:c + cols] = jnp.dot(xn, w_ref[:, col + c:col + c + cols], preferred_element_type=F32)
        col += width


def _to_bf16_kernel(w_ref, o_ref):
    o_ref[...] = w_ref[...].astype(BF16)


def _to_bf16_call(w, layer, rows=256):
    k, n = w.shape[1:]
    return pl.pallas_call(
        _to_bf16_kernel,
        grid=(k // rows,),
        in_specs=[pl.BlockSpec((None, rows, n), lambda i: (layer, i, 0))],
        out_specs=pl.BlockSpec((rows, n), lambda i: (i, 0)),
        out_shape=jax.ShapeDtypeStruct((k, n), BF16),
        compiler_params=_params(("arbitrary",)),
        name="weight_to_bf16",
    )(w)


def _qkv_call(x, g, w_in, layer):
    w_bf = _to_bf16_call(w_in, layer)
    return pl.pallas_call(
        _qkv_kernel,
        grid=(N_TOK // TM_QKV,),
        in_specs=[pl.BlockSpec((TM_QKV, D_MODEL), lambda m: (m, 0)),
                  pl.BlockSpec((1, D_MODEL), lambda m: (0, 0)),
                  pl.BlockSpec((D_MODEL, QKV_DIM), lambda m: (0, 0), pipeline_mode=pl.Buffered(1))],
        out_specs=[pl.BlockSpec((TM_QKV, w), lambda m: (m, 0)) for w in QKV_WIDTHS],
        out_shape=[jax.ShapeDtypeStruct((N_TOK, w), F32) for w in QKV_WIDTHS],
        compiler_params=_params(("arbitrary",)),
        name="qkv_proj",
    )(x, g.reshape(1, D_MODEL), w_bf)


def _mm_res_kernel(a_ref, w_ref, r_ref, o_ref, wbf_ref):
    @pl.when(pl.program_id(1) == 0)
    def _():
        _cast_weight(w_ref, wbf_ref)

    o_ref[...] = r_ref[...] + jnp.dot(a_ref[...], wbf_ref[...], preferred_element_type=F32)


def _mm_res_call(a, w, layer, res, tm, tn, name):
    m_rows, k = a.shape
    n_cols = w.shape[2]
    return pl.pallas_call(
        _mm_res_kernel,
        grid=(n_cols // tn, m_rows // tm),
        in_specs=[pl.BlockSpec((tm, k), lambda n, m: (m, 0)),
                  pl.BlockSpec((None, k, tn), lambda n, m: (layer, 0, n)),
                  pl.BlockSpec((tm, tn), lambda n, m: (m, n))],
        out_specs=pl.BlockSpec((tm, tn), lambda n, m: (m, n)),
        out_shape=jax.ShapeDtypeStruct((m_rows, n_cols), F32),
        scratch_shapes=[pltpu.VMEM((k, tn), BF16)],
        compiler_params=_params(("arbitrary", "arbitrary")),
        name=name,
    )(a, w, res)


def _silu_mul(g, u):
    return g * (1.0 / (1.0 + jnp.exp(-g))) * u


def _swiglu_kernel(a_ref, wg_ref, wu_ref, o_ref, wgbf_ref, wubf_ref):
    @pl.when(pl.program_id(1) == 0)
    def _():
        _cast_weight(wg_ref, wgbf_ref)
        _cast_weight(wu_ref, wubf_ref)

    a = a_ref[...]
    g = jnp.dot(a, wgbf_ref[...], preferred_element_type=F32)
    u = jnp.dot(a, wubf_ref[...], preferred_element_type=F32)
    o_ref[...] = _silu_mul(g, u).astype(o_ref.dtype)


def _swiglu_call(a, wg, wu, idx, tm=TM, tn=512):
    m_rows, k = a.shape
    n_cols = wg.shape[2]
    wspec = pl.BlockSpec((None, k, tn), lambda n, m: (idx, 0, n))
    return pl.pallas_call(
        _swiglu_kernel,
        grid=(n_cols // tn, m_rows // tm),
        in_specs=[pl.BlockSpec((tm, k), lambda n, m: (m, 0)), wspec, wspec],
        out_specs=pl.BlockSpec((tm, tn), lambda n, m: (m, n)),
        out_shape=jax.ShapeDtypeStruct((m_rows, n_cols), BF16),
        scratch_shapes=[pltpu.VMEM((k, tn), BF16), pltpu.VMEM((k, tn), BF16)],
        compiler_params=_params(("arbitrary", "arbitrary")),
        name="dense_gate_up",
    )(a, wg, wu)


def _t5_bucket_idx(dist):
    n = jnp.maximum(dist, 0)
    max_exact = NUM_BUCKETS // 2
    ratio = jnp.log(jnp.maximum(n, max_exact).astype(F32) / max_exact) / math.log(MAX_DISTANCE / max_exact)
    large = max_exact + (ratio * (NUM_BUCKETS - max_exact)).astype(jnp.int32)
    large = jnp.minimum(large, NUM_BUCKETS - 1)
    return jnp.where(n < max_exact, n, large)


def _bias_kernel(idx_ref, tab_ref, o_ref):
    idx = idx_ref[...]
    acc = jnp.full(idx.shape, NEG_INF, F32)
    for b in range(NUM_BUCKETS):
        acc = jnp.where(idx == b, tab_ref[b:b + 1, :], acc)
    o_ref[...] = acc


def _bias_call(idx, tab_rows):
    groups = tab_rows.shape[0]
    rows, cols = idx.shape
    return pl.pallas_call(
        _bias_kernel,
        grid=(groups,),
        in_specs=[pl.BlockSpec((rows, cols), lambda g: (0, 0)),
                  pl.BlockSpec((None, NUM_BUCKETS, cols), lambda g: (g, 0, 0))],
        out_specs=pl.BlockSpec((None, rows, cols), lambda g: (g, 0, 0)),
        out_shape=jax.ShapeDtypeStruct((groups, rows, cols), F32),
        compiler_params=_params(("arbitrary",)),
        name="rel_bias",
    )(idx, tab_rows)


def _band_bucket_idx(step, max_dist):
    i = lax.broadcasted_iota(jnp.int32, (BLOCK, 2 * BLOCK), 0)
    c = lax.broadcasted_iota(jnp.int32, (BLOCK, 2 * BLOCK), 1)
    dist = (i - c + BLOCK) * step
    valid = (dist >= 0) & (dist <= max_dist)
    return jnp.where(valid, _t5_bucket_idx(dist), -1)


def _dilation_multiplicity(dist):
    mult = jnp.zeros(dist.shape, jnp.int32)
    for w, r in DILATED:
        mult = mult + ((dist >= 0) & (dist <= w) & (dist % r == 0)).astype(jnp.int32)
    return mult


def _lane_half_mask():
    return lax.broadcasted_iota(jnp.int32, (BLOCK, LANES), 1) < HEAD_DIM


def _dup_head(t, kv):
    rolled = pltpu.roll(t, HEAD_DIM, axis=1)
    low = lax.broadcasted_iota(jnp.int32, t.shape, 1) < HEAD_DIM
    return jnp.where(low, t, rolled) if kv == 0 else jnp.where(low, rolled, t)


def _nt_dot(a, b):
    return lax.dot_general(a, b, (((1,), (1,)), ((), ())), preferred_element_type=F32)


def _band_scores(q2, kp, kc, bias, heads, first):
    low = _lane_half_mask()
    out = []
    for half, h in enumerate(heads):
        qh = jnp.where(low if half == 0 else ~low, q2, 0.0).astype(BF16)
        bias_prev = bias[h, :, 0:BLOCK]
        if first is not False:
            bias_prev = jnp.where(first, NEG_INF, bias_prev)
        out.append((_nt_dot(qh, kp) + bias_prev, _nt_dot(qh, kc) + bias[h, :, BLOCK:2 * BLOCK]))
    return out


def _band_weights(scores, heads, sink_ref):
    out = []
    for (sp, sc), h in zip(scores, heads):
        m = jnp.max(jnp.maximum(sp, sc), axis=1, keepdims=True)
        if sink_ref is not None:
            m = jnp.maximum(m, sink_ref[h])
        pp = jnp.exp(sp - m)
        pc = jnp.exp(sc - m)
        l = jnp.sum(pp + pc, axis=1, keepdims=True)
        if sink_ref is not None:
            l = l + jnp.exp(sink_ref[h] - m)
        out.append((m, l, pp.astype(BF16), pc.astype(BF16)))
    return out


def _band_values(weights, vp, vc):
    low = _lane_half_mask()
    os_ = [jnp.dot(pp, vp, preferred_element_type=F32) + jnp.dot(pc, vc, preferred_element_type=F32)
           for _, _, pp, pc in weights]
    pick = lambda a, b: jnp.where(low, a, b)
    return pick(weights[0][0], weights[1][0]), pick(weights[0][1], weights[1][1]), pick(os_[0], os_[1])


SCORE_LOOKAHEAD = 2


def _pipelined_pairs(n_items, scores_of, values_of, finish, sink_ref=None):
    pending = {}
    for i in range(n_items + SCORE_LOOKAHEAD):
        if i < n_items:
            q2, kp, kc, bias, heads, first = scores_of(i)
            pending[i] = (heads, _band_scores(q2, kp, kc, bias, heads, first))
        j = i - SCORE_LOOKAHEAD
        if j >= 0:
            heads, scores = pending.pop(j)
            finish(j, *_band_values(_band_weights(scores, heads, sink_ref), *values_of(j)))


def _swa_kernel(sink_ref, q_ref, kvp_ref, kvc_ref, bm_ref, o_ref):
    first = pl.program_id(1) == 0
    dup = lambda ref, cols: [_dup_head(ref[:, cols], kv).astype(BF16) for kv in range(N_KV_A)]
    kprev, kcur = dup(kvp_ref, slice(0, LANES)), dup(kvc_ref, slice(0, LANES))
    vprev, vcur = dup(kvp_ref, slice(LANES, 2 * LANES)), dup(kvc_ref, slice(LANES, 2 * LANES))
    cols = lambda p: slice(p * LANES, (p + 1) * LANES)
    kv_of = lambda p: (2 * p) // (N_HEADS_A // N_KV_A)

    def scores_of(p):
        return q_ref[:, cols(p)] * ATTN_SCALE, kprev[kv_of(p)], kcur[kv_of(p)], bm_ref, (2 * p, 2 * p + 1), first

    def finish(p, m, l, o):
        o_ref[:, cols(p)] = o / l

    _pipelined_pairs(WIDTH_A // LANES, scores_of, lambda p: (vprev[kv_of(p)], vcur[kv_of(p)]), finish, sink_ref)


def _swa_prompt_call(qa, kva, bm, sink):
    nb = SEQ // BLOCK
    cur = lambda b, n: (b * nb + n, 0)
    prev = lambda b, n: (b * nb + jnp.maximum(n - 1, 0), 0)
    return pl.pallas_call(
        _swa_kernel,
        grid=(BATCH, nb),
        in_specs=[pl.BlockSpec(memory_space=pltpu.SMEM),
                  pl.BlockSpec((BLOCK, WIDTH_A), cur),
                  pl.BlockSpec((BLOCK, WIDTH_KVA), prev),
                  pl.BlockSpec((BLOCK, WIDTH_KVA), cur),
                  pl.BlockSpec((N_HEADS_A, BLOCK, 2 * BLOCK), lambda b, n: (0, 0, 0))],
        out_specs=pl.BlockSpec((BLOCK, WIDTH_A), cur),
        out_shape=jax.ShapeDtypeStruct((N_PROMPT, WIDTH_A), F32),
        compiler_params=_params(("arbitrary", "arbitrary")),
        name="swa_prompt",
    )(sink, qa, kva, kva, bm)


SUPER = BLOCK * DILATED[-1][1]
HEADS_PER_STEP = 2
STEP_WIDTH = HEADS_PER_STEP * HEAD_DIM
BLOCKS_PER_STEP = 4
SUBLANES = 8


def _dilated_kernel(q_ref, k_ref, v_ref, bm_ref, o_ref, kwin, vwin, m_s, l_s, acc_s):
    n = pl.program_id(2)

    @pl.when(n == 0)
    def _():
        kwin[0:SUPER, :] = jnp.zeros((SUPER, STEP_WIDTH), F32)
        vwin[0:SUPER, :] = jnp.zeros((SUPER, STEP_WIDTH), F32)

    @pl.when(n > 0)
    def _():
        kwin[0:SUPER, :] = kwin[SUPER:2 * SUPER, :]
        vwin[0:SUPER, :] = vwin[SUPER:2 * SUPER, :]

    kwin[SUPER:2 * SUPER, :] = k_ref[...]
    vwin[SUPER:2 * SUPER, :] = v_ref[...]

    def sweep(d, r, starts, firsts):
        last = d == len(DILATED) - 1
        rows = lambda s: pl.ds(s, BLOCK, stride=r) if r > 1 else pl.ds(s, BLOCK)

        def scores_of(i):
            q2 = q_ref[rows(starts[i]), :] * ATTN_SCALE
            kp = kwin[rows(SUPER + starts[i] - BLOCK * r), :].astype(BF16)
            kc = kwin[rows(SUPER + starts[i]), :].astype(BF16)
            return q2, kp, kc, bm_ref.at[d], (0, 1), firsts[i]

        def values_of(i):
            return (vwin[rows(SUPER + starts[i] - BLOCK * r), :].astype(BF16),
                    vwin[rows(SUPER + starts[i]), :].astype(BF16))

        def finish(i, m, l, o):
            at = rows(starts[i])
            if d > 0:
                m_old = m_s[at, :]
                m_new = jnp.maximum(m_old, m)
                a_old = jnp.exp(m_old - m_new)
                a_new = jnp.exp(m - m_new)
                l = a_old * l_s[at, :] + a_new * l
                o = a_old * acc_s[at, :] + a_new * o
                m = m_new
            if last:
                o_ref[at, :] = o / l
            else:
                m_s[at, :] = m
                l_s[at, :] = l
                acc_s[at, :] = o

        _pipelined_pairs(len(starts), scores_of, values_of, finish)

    for d, (_, r) in enumerate(DILATED):
        span = BLOCK * r
        per_step = max(BLOCKS_PER_STEP, min(r, SUBLANES))
        if r >= per_step:
            groups = r // per_step
            assert groups == 1 or per_step % SUBLANES == 0
            offsets = tuple(range(per_step))
            base_of = lambda i, groups=groups, span=span, per_step=per_step: (
                pl.multiple_of((i // groups) * span + (i % groups) * per_step, SUBLANES))
            firsts_of = lambda i, groups=groups, per_step=per_step: [(n == 0) & (i // groups == 0)] * per_step
            steps = (SUPER // span) * groups
        else:
            offsets = tuple(k * span for k in range(per_step))
            base_of = lambda i, span=span, per_step=per_step: pl.multiple_of(i * span * per_step, SUBLANES)
            firsts_of = lambda i, per_step=per_step: [(n == 0) & (i == 0)] + [False] * (per_step - 1)
            steps = SUPER // (span * per_step)

        def body(i, c, d=d, r=r, offsets=offsets, base_of=base_of, firsts_of=firsts_of):
            base = base_of(i)
            sweep(d, r, [base + off for off in offsets], firsts_of(i))
            return c

        lax.fori_loop(0, steps, body, 0)


def _dilated_prompt_call(qb, kb, vb, bm):
    tiles = SEQ // SUPER
    tile = pl.BlockSpec((SUPER, STEP_WIDTH), lambda b, g, n: (b * tiles + n, g))
    return pl.pallas_call(
        _dilated_kernel,
        grid=(BATCH, N_HEADS_B // HEADS_PER_STEP, tiles),
        in_specs=[tile, tile, tile,
                  pl.BlockSpec((len(DILATED), HEADS_PER_STEP, BLOCK, 2 * BLOCK), lambda b, g, n: (0, g, 0, 0))],
        out_specs=tile,
        out_shape=jax.ShapeDtypeStruct((N_PROMPT, WIDTH_B), F32),
        scratch_shapes=[pltpu.VMEM((2 * SUPER, STEP_WIDTH), F32), pltpu.VMEM((2 * SUPER, STEP_WIDTH), F32),
                        pltpu.VMEM((SUPER, STEP_WIDTH), F32), pltpu.VMEM((SUPER, STEP_WIDTH), F32),
                        pltpu.VMEM((SUPER, STEP_WIDTH), F32)],
        compiler_params=_params(("arbitrary", "arbitrary", "arbitrary")),
        name="dilated_prompt",
    )(qb, kb, vb, bm)


TM_OUT = 256


def _out_proj_kernel(oa_ref, ob_ref, ys_ref, ga_ref, gb_ref, w_ref, x_ref, o_ref, wbf_ref):
    m = pl.program_id(0)

    @pl.when(m == 0)
    def _():
        _cast_weight(w_ref, wbf_ref)

    def project(ya, yb):
        o_ref[...] = (x_ref[...]
                      + jnp.dot(ya.astype(BF16), wbf_ref[0:WIDTH_A, :], preferred_element_type=F32)
                      + jnp.dot(yb.astype(BF16), wbf_ref[WIDTH_A:, :], preferred_element_type=F32))

    @pl.when(m < N_PROMPT // TM_OUT)
    def _():
        project(_rmsnorm_rows(oa_ref[...], ga_ref[...]), _rmsnorm_rows(ob_ref[...], gb_ref[...]))

    @pl.when(m >= N_PROMPT // TM_OUT)
    def _():
        project(ys_ref[:, 0:WIDTH_A], ys_ref[:, WIDTH_A:])


def _out_proj_call(oa, ob, y_sample, ga, gb, w_out, layer, x):
    assert N_SAMPLE == TM_OUT
    last_prompt = N_PROMPT // TM_OUT - 1
    prompt = lambda w: pl.BlockSpec((TM_OUT, w), lambda m: (jnp.minimum(m, last_prompt), 0))
    const = lambda *shape: pl.BlockSpec(shape, lambda m: (0,) * len(shape))
    rows = pl.BlockSpec((TM_OUT, D_MODEL), lambda m: (m, 0))
    width = WIDTH_A + WIDTH_B
    return pl.pallas_call(
        _out_proj_kernel,
        grid=(N_TOK // TM_OUT,),
        in_specs=[prompt(WIDTH_A), prompt(WIDTH_B), const(N_SAMPLE, width), const(1, WIDTH_A), const(1, WIDTH_B),
                  pl.BlockSpec((None, width, D_MODEL), lambda m: (layer, 0, 0), pipeline_mode=pl.Buffered(1)),
                  rows],
        out_specs=rows,
        out_shape=jax.ShapeDtypeStruct((N_TOK, D_MODEL), F32),
        scratch_shapes=[pltpu.VMEM((width, D_MODEL), BF16)],
        compiler_params=_params(("arbitrary",)),
        name="out_proj",
    )(oa, ob, y_sample, ga.reshape(1, -1), gb.reshape(1, -1), w_out, x)


NEW_ROWS = 128
SINK_ROW = DEC_SEQ
HEAD_GROUPS = 2
GROUP_WIDTH = WIDTH_B // HEAD_GROUPS


def _pad_rows(t, rows):
    return jnp.concatenate([t, jnp.zeros((rows - t.shape[0], t.shape[1]), t.dtype)], axis=0)


def _sample_window_a(qa_ref, kva_ref, cak_ref, cav_ref, bac_ref, ban_ref, sink_ref):
    group = N_HEADS_A // N_KV_A
    lane8 = lax.broadcasted_iota(jnp.int32, (DEC_SEQ, LANES), 1) < HEAD_DIM
    rows = []
    for p in range(WIDTH_A // LANES):
        blk = qa_ref[:, p * LANES:(p + 1) * LANES] * ATTN_SCALE
        rolled = pltpu.roll(blk, HEAD_DIM, axis=1)
        if (2 * p) // group == 0:
            rows += [jnp.where(lane8, blk, 0.0), jnp.where(lane8, rolled, 0.0)]
        else:
            rows += [jnp.where(lane8, 0.0, rolled), jnp.where(lane8, 0.0, blk)]
    qat = jnp.concatenate(rows, axis=0).astype(BF16)
    kan = _pad_rows(kva_ref[:, 0:LANES], NEW_ROWS)
    van = _pad_rows(kva_ref[:, LANES:2 * LANES], NEW_ROWS)
    row_id = lax.broadcasted_iota(jnp.int32, (NEW_ROWS, LANES), 0)
    sc = _nt_dot(cak_ref[...].astype(BF16), qat) + bac_ref[...]
    sn = _nt_dot(kan.astype(BF16), qat) + jnp.where(row_id == SINK_ROW, sink_ref[...], ban_ref[...])
    m = jnp.maximum(jnp.max(sc, axis=0, keepdims=True), jnp.max(sn, axis=0, keepdims=True))
    pct = jnp.exp(sc - m).T
    pnt = jnp.exp(sn - m).T
    l_col = jnp.sum(pct, axis=1, keepdims=True) + jnp.sum(pnt, axis=1, keepdims=True)
    pct = pct.astype(BF16)
    pnt = pnt.astype(BF16)
    res = []
    for kv in range(N_KV_A):
        vc = _dup_head(cav_ref[...], kv).astype(BF16)
        vn = _dup_head(van, kv).astype(BF16)
        res.append((jnp.dot(pct, vc, preferred_element_type=F32)
                    + jnp.dot(pnt, vn, preferred_element_type=F32)) / l_col)
    pairs = []
    for p in range(WIDTH_A // LANES):
        r = res[(2 * p) // group]
        pairs.append(jnp.where(lane8, r[16 * p:16 * p + 8, :], r[16 * p + 8:16 * p + 16, :]))
    return jnp.concatenate(pairs, axis=1)


KEEP_LANES = LANES - DEC_SEQ


def _shift_lanes(old_ref, out_ref, tail):
    feats, window = old_ref.shape
    lane = lax.broadcasted_iota(jnp.int32, (feats, LANES), 1)
    nxt = tail
    for j in reversed(range(window // LANES)):
        cur = pltpu.roll(old_ref[:, j * LANES:(j + 1) * LANES], KEEP_LANES, axis=1)
        out_ref[:, j * LANES:(j + 1) * LANES] = jnp.where(lane < KEEP_LANES, cur, nxt)
        nxt = cur


def _shift_in_new(old_ref, new_ref, out_ref):
    tails = []
    for c in range(old_ref.shape[0] // LANES):
        sq = _pad_rows(new_ref[:, c * LANES:(c + 1) * LANES], LANES).T
        tails.append(pltpu.roll(sq, KEEP_LANES, axis=1))
    _shift_lanes(old_ref, out_ref, jnp.concatenate(tails, axis=0))


def _sample_kernel(qa_ref, kva_ref, qb_ref, kb_ref, vb_ref, cak_ref, cav_ref, cbk_ref, cbv_ref,
                   bac_ref, ban_ref, sink_ref, bbc_ref, bbn_ref, mbc_ref, mbn_ref, ga_ref, gb_ref,
                   o_ref, oa_s, ob_s):
    g = pl.program_id(1)

    @pl.when(g == 0)
    def _():
        oa_s[...] = _sample_window_a(qa_ref, kva_ref, cak_ref, cav_ref, bac_ref, ban_ref, sink_ref)

    low = lax.broadcasted_iota(jnp.int32, (DEC_SEQ, LANES), 1) < HEAD_DIM
    n_pairs = GROUP_WIDTH // LANES
    cols = lambda p: slice(p * LANES, (p + 1) * LANES)

    def scores(p):
        q2 = qb_ref[:, cols(p)] * ATTN_SCALE
        kt = cbk_ref[cols(p), :].astype(BF16)
        kn = _pad_rows(kb_ref[:, cols(p)], NEW_ROWS).astype(BF16)
        out = []
        for half in range(2):
            qh = jnp.where(low if half == 0 else ~low, q2, 0.0).astype(BF16)
            out.append((jnp.dot(qh, kt, preferred_element_type=F32) + bbc_ref[2 * p + half],
                        _nt_dot(qh, kn) + bbn_ref[2 * p + half]))
        return out

    def values(p, pair_scores):
        vt = cbv_ref[cols(p), :].astype(BF16)
        vn = _pad_rows(vb_ref[:, cols(p)], NEW_ROWS).astype(BF16)
        halves = []
        for sc, sn in pair_scores:
            m = jnp.maximum(jnp.max(sc, axis=1, keepdims=True), jnp.max(sn, axis=1, keepdims=True))
            pc = jnp.exp(sc - m) * mbc_ref[...]
            pn = jnp.exp(sn - m) * mbn_ref[...]
            l = jnp.sum(pc, axis=1, keepdims=True) + jnp.sum(pn, axis=1, keepdims=True)
            o = _nt_dot(pc.astype(BF16), vt) + jnp.dot(pn.astype(BF16), vn, preferred_element_type=F32)
            halves.append(o / l)
        return jnp.where(low, halves[0], halves[1])

    pending, pairs = {}, []
    for p in range(n_pairs + SCORE_LOOKAHEAD):
        if p < n_pairs:
            pending[p] = scores(p)
        if p >= SCORE_LOOKAHEAD:
            pairs.append(values(p - SCORE_LOOKAHEAD, pending.pop(p - SCORE_LOOKAHEAD)))
    ob = jnp.concatenate(pairs, axis=1)
    for k in range(HEAD_GROUPS):
        @pl.when(g == k)
        def _(k=k):
            ob_s[:, k * GROUP_WIDTH:(k + 1) * GROUP_WIDTH] = ob

    @pl.when(g == HEAD_GROUPS - 1)
    def _():
        o_ref[:, 0:WIDTH_A] = _rmsnorm_rows(oa_s[...], ga_ref[...])
        o_ref[:, WIDTH_A:] = _rmsnorm_rows(ob_s[...], gb_ref[...])


def _sample_call(qa, kva, qb, kb, vb, cak, cav, cbk_t, cbv_t, consts, sink_row, ga, gb, layer):
    first = N_PROMPT // DEC_SEQ
    new = lambda w: pl.BlockSpec((DEC_SEQ, w), lambda b, g: (first + b, 0))
    new_g = pl.BlockSpec((DEC_SEQ, GROUP_WIDTH), lambda b, g: (first + b, g))
    la, lb = cak.shape[2], cbk_t.shape[3]
    cache_a = pl.BlockSpec((None, None, la, LANES), lambda b, g: (layer, b, 0, 0))
    cache_b = pl.BlockSpec((None, None, GROUP_WIDTH, lb), lambda b, g: (layer, b, g, 0))
    const = lambda *shape: pl.BlockSpec(shape, lambda b, g: (0,) * len(shape))
    heads = N_HEADS_B // HEAD_GROUPS
    per_head = lambda *shape: pl.BlockSpec((heads,) + shape, lambda b, g: (g,) + (0,) * len(shape))
    in_specs = [new(WIDTH_A), new(WIDTH_KVA), new_g, new_g, new_g, cache_a, cache_a, cache_b, cache_b,
                const(la, LANES), const(NEW_ROWS, LANES), const(1, LANES),
                per_head(DEC_SEQ, lb), per_head(DEC_SEQ, NEW_ROWS), const(DEC_SEQ, lb), const(DEC_SEQ, NEW_ROWS),
                const(1, WIDTH_A), const(1, WIDTH_B)]
    args = [qa, kva, qb, kb, vb, cak, cav, cbk_t, cbv_t,
            consts["bac"], consts["ban"], sink_row, consts["bbc"], consts["bbn"], consts["mbc"], consts["mbn"],
            ga.reshape(1, -1), gb.reshape(1, -1)]
    return pl.pallas_call(
        _sample_kernel,
        grid=(DEC_BATCH, HEAD_GROUPS),
        in_specs=in_specs,
        out_specs=pl.BlockSpec((DEC_SEQ, WIDTH_A + WIDTH_B), lambda b, g: (b, 0)),
        out_shape=jax.ShapeDtypeStruct((N_SAMPLE, WIDTH_A + WIDTH_B), F32),
        scratch_shapes=[pltpu.VMEM((DEC_SEQ, WIDTH_A), F32), pltpu.VMEM((DEC_SEQ, WIDTH_B), F32)],
        compiler_params=_params(("arbitrary", "arbitrary")),
        name="sample_attention",
    )(*args)


TAIL_ENTRIES = 8


def _cache_tail_kernel(old_ref, new_ref, shifted_hbm, out_ref):
    del shifted_hbm
    for i in range(TAIL_ENTRIES):
        _shift_in_new(old_ref.at[i], new_ref.at[pl.ds(i * DEC_SEQ, DEC_SEQ), :], out_ref.at[i])


def _cache_tail_call(cache_t, shifted, new_rows):
    last_tile = cache_t.shape[3] // LANES - 1
    tile = pl.BlockSpec((None, TAIL_ENTRIES, WIDTH_B, LANES), lambda l, b: (l, b, 0, last_tile))
    return pl.pallas_call(
        _cache_tail_kernel,
        grid=(DEPTH, DEC_BATCH // TAIL_ENTRIES),
        in_specs=[tile, pl.BlockSpec((None, TAIL_ENTRIES * DEC_SEQ, WIDTH_B), lambda l, b: (l, b, 0)),
                  pl.BlockSpec(memory_space=pl.ANY)],
        out_specs=tile,
        out_shape=jax.ShapeDtypeStruct(cache_t.shape, F32),
        input_output_aliases={2: 0},
        compiler_params=_params(("arbitrary", "arbitrary")),
        name="cache_tail",
    )(cache_t, new_rows, shifted)


def _sample_consts(rel_bias):
    la, lb = WINDOW_A, WINDOW_B

    def idx_of(dist, valid):
        return jnp.where(valid, _t5_bucket_idx(dist), -1)

    t = lax.broadcasted_iota(jnp.int32, (1, LANES), 1) % DEC_SEQ
    tab_a = jnp.repeat(rel_bias[:, :N_HEADS_A], DEC_SEQ, axis=1)[None]
    i_a = lax.broadcasted_iota(jnp.int32, (la, LANES), 0)
    dist = la + t - i_a
    bac = _bias_call(idx_of(dist, dist <= WINDOW_A - 1), tab_a)[0]
    j = lax.broadcasted_iota(jnp.int32, (NEW_ROWS, LANES), 0)
    dist_n = t - j
    ban = _bias_call(idx_of(dist_n, (dist_n >= 0) & (j < DEC_SEQ)), tab_a)[0]
    tab_b = rel_bias[:, N_HEADS_A:].T[:, :, None]
    tq = lax.broadcasted_iota(jnp.int32, (DEC_SEQ, lb), 0)
    ik = lax.broadcasted_iota(jnp.int32, (DEC_SEQ, lb), 1)
    dist_b = lb + tq - ik
    mult_c = _dilation_multiplicity(dist_b)
    bbc = _bias_call(idx_of(dist_b, mult_c > 0), jnp.broadcast_to(tab_b, (N_HEADS_B, NUM_BUCKETS, lb)))
    tn = lax.broadcasted_iota(jnp.int32, (DEC_SEQ, NEW_ROWS), 0)
    jn = lax.broadcasted_iota(jnp.int32, (DEC_SEQ, NEW_ROWS), 1)
    mult_n = jnp.where(jn < DEC_SEQ, _dilation_multiplicity(tn - jn), 0)
    bbn = _bias_call(idx_of(tn - jn, mult_n > 0), jnp.broadcast_to(tab_b, (N_HEADS_B, NUM_BUCKETS, NEW_ROWS)))
    return {"bac": bac, "ban": ban, "bbc": bbc, "bbn": bbn,
            "mbc": mult_c.astype(F32), "mbn": mult_n.astype(F32)}


def _router_kernel(x_ref, rt_ref, idx_ref, gate_ref):
    logits = lax.dot_general(rt_ref[...], x_ref[...], (((1,), (1,)), ((), ())),
                             precision=lax.Precision.HIGHEST, preferred_element_type=F32)
    e_id = lax.broadcasted_iota(jnp.int32, logits.shape, 0)
    m1 = jnp.max(logits, axis=0, keepdims=True)
    i1 = jnp.min(jnp.where(logits == m1, e_id, N_EXPERTS), axis=0, keepdims=True)
    rest = jnp.where(e_id == i1, -jnp.inf, logits)
    m2 = jnp.max(rest, axis=0, keepdims=True)
    i2 = jnp.min(jnp.where(rest == m2, e_id, N_EXPERTS), axis=0, keepdims=True)
    e2 = jnp.exp(m2 - m1)
    den = 1.0 + e2
    idx_ref[0:1, :] = i1
    idx_ref[1:2, :] = i2
    gate_ref[0:1, :] = 1.0 / den
    gate_ref[1:2, :] = e2 / den


def _router_call(xn, router_t, tm=TM):
    return pl.pallas_call(
        _router_kernel,
        grid=(N_TOK // tm,),
        in_specs=[pl.BlockSpec((tm, D_MODEL), lambda i: (i, 0)),
                  pl.BlockSpec((N_EXPERTS, D_MODEL), lambda i: (0, 0))],
        out_specs=[pl.BlockSpec((TOP_K, tm), lambda i: (0, i)),
                   pl.BlockSpec((TOP_K, tm), lambda i: (0, i))],
        out_shape=[jax.ShapeDtypeStruct((TOP_K, N_TOK), jnp.int32),
                   jax.ShapeDtypeStruct((TOP_K, N_TOK), F32)],
        compiler_params=_params(("arbitrary",)),
        name="moe_router",
    )(xn, router_t)


def _routing_plan(top_i):
    e = top_i.T.reshape(-1)
    order = jnp.argsort(e, stable=True)
    onehot = (e[:, None] == jnp.arange(N_EXPERTS)[None, :]).astype(jnp.int32)
    counts = jnp.sum(onehot, axis=0)
    starts = jnp.cumsum(counts) - counts
    padded = (counts + MOE_ROWS - 1) // MOE_ROWS * MOE_ROWS
    pstarts = jnp.cumsum(padded) - padded
    blk_start = jnp.arange(N_MOE_BLOCKS) * MOE_ROWS
    blk_e = jnp.minimum(jnp.sum(blk_start[:, None] >= (pstarts + padded)[None, :], axis=1), N_EXPERTS - 1)
    n_used = (jnp.sum(padded) // MOE_ROWS).astype(jnp.int32).reshape(1)
    row = jnp.arange(N_MOE_ROWS)
    row_e = blk_e[row // MOE_ROWS]
    within = row - pstarts[row_e]
    src = order[jnp.clip(starts[row_e] + within, 0, N_ASSIGN - 1)]
    row_tok = jnp.where(within < counts[row_e], src // TOP_K, 0).astype(jnp.int32)
    rank = jnp.take_along_axis(jnp.cumsum(onehot, axis=0) - onehot, e[:, None], axis=1)[:, 0]
    dest = (pstarts[e] + rank).astype(jnp.int32)
    dest_kt = dest.reshape(N_TOK, TOP_K).T.reshape(-1)
    return blk_e.astype(jnp.int32), n_used, row_tok, dest_kt


def _row_copy(src_hbm, row, dst, slot, sem):
    return pltpu.make_async_copy(
        src_hbm.at[pl.ds(pl.multiple_of(row * ROW_CHUNKS, ROW_CHUNKS), ROW_CHUNKS), :],
        dst.at[pl.ds(pl.multiple_of(slot * ROW_CHUNKS, ROW_CHUNKS), ROW_CHUNKS), :],
        sem)


def _wait_rows(src_hbm, dst, sem):
    pltpu.make_async_copy(src_hbm.at[pl.ds(0, dst.shape[0]), :], dst, sem).wait()


ISSUE_UNROLL = 8


def _gather_kernel(tok_ref, x_hbm, o_ref, buf0, buf1, sems):
    b = pl.program_id(0)
    bufs = (buf0, buf1)

    def request(block, slot):
        def issue(i, c):
            for u in range(ISSUE_UNROLL):
                row = i * ISSUE_UNROLL + u
                _row_copy(x_hbm, tok_ref[block * MOE_ROWS + row], bufs[slot], row, sems.at[slot]).start()
            return c

        lax.fori_loop(0, MOE_ROWS // ISSUE_UNROLL, issue, 0)

    @pl.when(b == 0)
    def _():
        request(0, 0)

    for slot in range(2):
        @pl.when((b + 1 < N_MOE_BLOCKS) & ((b + 1) % 2 == slot))
        def _(slot=slot):
            request(b + 1, slot)

    for slot in range(2):
        @pl.when(b % 2 == slot)
        def _(slot=slot):
            _wait_rows(x_hbm, bufs[slot], sems.at[slot])
            for j in range(ROW_CHUNKS):
                chunk = bufs[slot][pl.ds(j, MOE_ROWS, stride=ROW_CHUNKS), :]
                o_ref[:, j * LANES:(j + 1) * LANES] = chunk.astype(o_ref.dtype)


def _gather_call(row_tok, x_chunks):
    grid_spec = pltpu.PrefetchScalarGridSpec(
        num_scalar_prefetch=1,
        grid=(N_MOE_BLOCKS,),
        in_specs=[pl.BlockSpec(memory_space=pl.ANY)],
        out_specs=pl.BlockSpec((MOE_ROWS, D_MODEL), lambda b, tok: (b, 0)),
        scratch_shapes=[pltpu.VMEM((MOE_ROWS * ROW_CHUNKS, LANES), F32),
                        pltpu.VMEM((MOE_ROWS * ROW_CHUNKS, LANES), F32),
                        pltpu.SemaphoreType.DMA((2,))],
    )
    return pl.pallas_call(
        _gather_kernel,
        grid_spec=grid_spec,
        out_shape=jax.ShapeDtypeStruct((N_MOE_ROWS, D_MODEL), BF16),
        compiler_params=_params(("arbitrary",)),
        name="moe_dispatch",
    )(row_tok, x_chunks)


def _expert_changed(be_ref, nu_ref):
    b = pl.program_id(1)
    last = jnp.minimum(b, nu_ref[0] - 1)
    return (b < nu_ref[0]) & ((b == 0) | (be_ref[last] != be_ref[jnp.maximum(last - 1, 0)]))


SHIFT_ROWS = 256
SHIFT_BLOCKS = DEPTH * DEC_BATCH * (WIDTH_B // SHIFT_ROWS)


def _moe_up_kernel(be_ref, nu_ref, a_ref, wg_ref, wu_ref, cache_ref, o_ref, shifted_ref, wgbf_ref, wubf_ref):
    @pl.when(_expert_changed(be_ref, nu_ref))
    def _():
        _cast_weight(wg_ref, wgbf_ref)
        _cast_weight(wu_ref, wubf_ref)

    used = pl.program_id(1) < nu_ref[0]
    shift = lambda: _shift_lanes(cache_ref, shifted_ref, jnp.zeros((SHIFT_ROWS, LANES), F32))

    @pl.when(used)
    def _():
        a = a_ref[...]
        g = jnp.dot(a, wgbf_ref[...], preferred_element_type=F32)
        u = jnp.dot(a, wubf_ref[...], preferred_element_type=F32)
        o_ref[...] = _silu_mul(g, u).astype(o_ref.dtype)
        shift()

    @pl.when(jnp.logical_not(used))
    def _():
        o_ref[...] = jnp.zeros(o_ref.shape, o_ref.dtype)
        shift()


def _moe_down_kernel(be_ref, nu_ref, a_ref, w_ref, o_ref, wbf_ref):
    @pl.when(_expert_changed(be_ref, nu_ref))
    def _():
        _cast_weight(w_ref, wbf_ref)

    used = pl.program_id(1) < nu_ref[0]

    @pl.when(used)
    def _():
        o_ref[...] = jnp.dot(a_ref[...], wbf_ref[...], preferred_element_type=F32)

    @pl.when(jnp.logical_not(used))
    def _():
        o_ref[...] = jnp.zeros(o_ref.shape, o_ref.dtype)


def _moe_specs(k, tn, idx):
    last = lambda b, nu: jnp.minimum(b, nu[0] - 1)
    a_spec = pl.BlockSpec((MOE_ROWS, k), lambda n, b, be, nu: (last(b, nu), 0))
    w_spec = pl.BlockSpec((None, None, k, tn), lambda n, b, be, nu: (idx, be[last(b, nu)], 0, n))
    o_spec = pl.BlockSpec((MOE_ROWS, tn), lambda n, b, be, nu: (b, n))
    return a_spec, w_spec, o_spec


def _moe_up_call(blk_e, n_used, a, wg, wu, idx, cache_t, tn=1024):
    k, n_cols = a.shape[1], wg.shape[3]
    a_spec, w_spec, o_spec = _moe_specs(k, tn, idx)
    assert (n_cols // tn) * N_MOE_BLOCKS >= SHIFT_BLOCKS
    per_entry = WIDTH_B // SHIFT_ROWS

    def cache_block(n, b, be, nu):
        s = jnp.minimum(n * N_MOE_BLOCKS + b, SHIFT_BLOCKS - 1)
        return (s // (DEC_BATCH * per_entry), (s // per_entry) % DEC_BATCH, s % per_entry, 0)

    cache_spec = pl.BlockSpec((None, None, SHIFT_ROWS, cache_t.shape[3]), cache_block)
    grid_spec = pltpu.PrefetchScalarGridSpec(
        num_scalar_prefetch=2,
        grid=(n_cols // tn, N_MOE_BLOCKS),
        in_specs=[a_spec, w_spec, w_spec, cache_spec],
        out_specs=[o_spec, cache_spec],
        scratch_shapes=[pltpu.VMEM((k, tn), BF16)] * 2,
    )
    return pl.pallas_call(
        _moe_up_kernel,
        grid_spec=grid_spec,
        out_shape=[jax.ShapeDtypeStruct((N_MOE_ROWS, n_cols), BF16), jax.ShapeDtypeStruct(cache_t.shape, F32)],
        compiler_params=_params(("arbitrary", "arbitrary")),
        name="moe_gate_up",
    )(blk_e, n_used, a, wg, wu, cache_t)


def _moe_down_call(blk_e, n_used, a, wd, idx, tn=512):
    k, n_cols = a.shape[1], wd.shape[3]
    a_spec, w_spec, o_spec = _moe_specs(k, tn, idx)
    grid_spec = pltpu.PrefetchScalarGridSpec(
        num_scalar_prefetch=2,
        grid=(n_cols // tn, N_MOE_BLOCKS),
        in_specs=[a_spec, w_spec],
        out_specs=o_spec,
        scratch_shapes=[pltpu.VMEM((k, tn), BF16)],
    )
    return pl.pallas_call(
        _moe_down_kernel,
        grid_spec=grid_spec,
        out_shape=jax.ShapeDtypeStruct((N_MOE_ROWS, n_cols), F32),
        compiler_params=_params(("arbitrary", "arbitrary")),
        name="moe_down",
    )(blk_e, n_used, a, wd)


COMBINE_ROWS = 256


def _combine_kernel(dest_ref, x_ref, g_ref, y_hbm, o_ref, *scratch):
    step = pl.program_id(0)
    sems = scratch[-1]
    bufs = (scratch[0:TOP_K], scratch[TOP_K:2 * TOP_K])

    def request(at, slot):
        def issue(i, c):
            for u in range(ISSUE_UNROLL):
                row = i * ISSUE_UNROLL + u
                for k in range(TOP_K):
                    src = dest_ref[k * N_TOK + at * COMBINE_ROWS + row]
                    _row_copy(y_hbm, src, bufs[slot][k], row, sems.at[slot * TOP_K + k]).start()
            return c

        lax.fori_loop(0, COMBINE_ROWS // ISSUE_UNROLL, issue, 0)

    @pl.when(step == 0)
    def _():
        request(0, 0)

    for slot in range(2):
        @pl.when((step + 1 < N_TOK // COMBINE_ROWS) & ((step + 1) % 2 == slot))
        def _(slot=slot):
            request(step + 1, slot)

    for slot in range(2):
        @pl.when(step % 2 == slot)
        def _(slot=slot):
            for k in range(TOP_K):
                _wait_rows(y_hbm, bufs[slot][k], sems.at[slot * TOP_K + k])
            g0 = g_ref[:, 0:1]
            g1 = g_ref[:, 1:2]
            for j in range(ROW_CHUNKS):
                cols = slice(j * LANES, (j + 1) * LANES)
                rows = pl.ds(j, COMBINE_ROWS, stride=ROW_CHUNKS)
                o_ref[:, cols] = x_ref[:, cols] + (g0 * bufs[slot][0][rows, :] + g1 * bufs[slot][1][rows, :])


def _combine_call(dest_kt, x, gates, y_chunks):
    grid_spec = pltpu.PrefetchScalarGridSpec(
        num_scalar_prefetch=1,
        grid=(N_TOK // COMBINE_ROWS,),
        in_specs=[pl.BlockSpec((COMBINE_ROWS, D_MODEL), lambda i, d: (i, 0)),
                  pl.BlockSpec((COMBINE_ROWS, TOP_K), lambda i, d: (i, 0)),
                  pl.BlockSpec(memory_space=pl.ANY)],
        out_specs=pl.BlockSpec((COMBINE_ROWS, D_MODEL), lambda i, d: (i, 0)),
        scratch_shapes=[pltpu.VMEM((COMBINE_ROWS * ROW_CHUNKS, LANES), F32)] * (2 * TOP_K)
        + [pltpu.SemaphoreType.DMA((2 * TOP_K,))],
    )
    return pl.pallas_call(
        _combine_kernel,
        grid_spec=grid_spec,
        out_shape=jax.ShapeDtypeStruct((N_TOK, D_MODEL), F32),
        compiler_params=_params(("arbitrary",)),
        name="moe_combine",
    )(dest_kt, x, gates, y_chunks)


def _moe_layer(x, g, router, wg, wu, wd, idx, cache_t):
    xn = _norm_call(x, g, F32)
    top_i, gates = _router_call(xn, router[idx].T)
    blk_e, n_used, row_tok, dest_kt = _routing_plan(top_i)
    xs = _gather_call(row_tok, xn.reshape(N_TOK * ROW_CHUNKS, LANES))
    h, shifted = _moe_up_call(blk_e, n_used, xs, wg, wu, idx, cache_t)
    y = _moe_down_call(blk_e, n_used, h, wd, idx)
    return _combine_call(dest_kt, x, gates.T, y.reshape(N_MOE_ROWS * ROW_CHUNKS, LANES)), shifted


def kernel(x_prompt, x_sample, cache_a_k, cache_a_v, cache_b_k, cache_b_v, g_mix_in, w_in, sinks, rel_bias,
           g_out_a, g_out_b, w_out, g_ffn, w_gate_d, w_up_d, w_down_d, router, w_gate_e, w_up_e, w_down_e,
           g_final):
    la, lb = cache_a_k.shape[2], cache_b_k.shape[2]
    x = jnp.concatenate([x_prompt.reshape(N_PROMPT, D_MODEL), x_sample.reshape(N_SAMPLE, D_MODEL)], axis=0)
    cak = cache_a_k.reshape(DEPTH, DEC_BATCH, la, N_KV_A * HEAD_DIM)
    cav = cache_a_v.reshape(DEPTH, DEC_BATCH, la, N_KV_A * HEAD_DIM)
    to_t = lambda c: jnp.transpose(c, (0, 1, 3, 4, 2)).reshape(DEPTH, DEC_BATCH, WIDTH_B, lb)
    from_t = lambda c: jnp.transpose(c.reshape(DEPTH, DEC_BATCH, N_HEADS_B, HEAD_DIM, lb), (0, 1, 4, 2, 3))
    cbk_t, cbv_t = to_t(cache_b_k), to_t(cache_b_v)

    head_rows = lambda tab: jnp.broadcast_to(tab.T[:, :, None], (tab.shape[1], NUM_BUCKETS, 2 * BLOCK))
    tab_a, tab_b = rel_bias[:, :N_HEADS_A], rel_bias[:, N_HEADS_A:]
    bm_a = _bias_call(_band_bucket_idx(1, WINDOW_A - 1), head_rows(tab_a))
    bm_b = jnp.stack([_bias_call(_band_bucket_idx(r, w), head_rows(tab_b)) for w, r in DILATED])
    consts = _sample_consts(rel_bias)

    tail = lambda t, keep: jnp.stack([t[(b + 1) * SEQ - keep:(b + 1) * SEQ] for b in range(BATCH)])
    kva_p, kb_p, vb_p, kva_s, kb_s, vb_s, shifted_caches = [], [], [], [], [], [], []
    for l in range(DEPTH):
        qa, kva, qb, kb, vb = _qkv_call(x, g_mix_in[l], w_in, l)
        oa = _swa_prompt_call(qa, kva, bm_a, sinks[l])
        ob = _dilated_prompt_call(qb, kb, vb, bm_b)
        sink_row = jnp.repeat(sinks[l], DEC_SEQ).reshape(1, LANES)
        y_sample = _sample_call(qa, kva, qb, kb, vb, cak, cav, cbk_t, cbv_t, consts, sink_row,
                                g_out_a[l], g_out_b[l], l)
        x = _out_proj_call(oa, ob, y_sample, g_out_a[l], g_out_b[l], w_out, l, x)
        if l % 2 == 0:
            hn = _norm_call(x, g_ffn[l], BF16)
            h = _swiglu_call(hn, w_gate_d, w_up_d, l // 2)
            x = _mm_res_call(h, w_down_d, l // 2, x, 384, 512, "dense_down")
        else:
            x, shifted = _moe_layer(x, g_ffn[l], router, w_gate_e, w_up_e, w_down_e, l // 2,
                                    (cbk_t, cbv_t)[(l // 2) % 2])
            shifted_caches.append(shifted)
        kva_p.append(tail(kva, min(WINDOW_A, SEQ)))
        kb_p.append(tail(kb, min(WINDOW_B, SEQ)))
        vb_p.append(tail(vb, min(WINDOW_B, SEQ)))
        kva_s.append(kva[N_PROMPT:])
        kb_s.append(kb[N_PROMPT:])
        vb_s.append(vb[N_PROMPT:])

    y = _norm_call(x, g_final, F32)
    y_prompt = y[:N_PROMPT].reshape(BATCH, SEQ, D_MODEL)
    y_sample = y[N_PROMPT:].reshape(DEC_BATCH, DEC_SEQ, D_MODEL)

    kva_p, kva_s = jnp.stack(kva_p), jnp.stack(kva_s)
    heads_a = lambda t: t.reshape(t.shape[:-1] + (N_KV_A, HEAD_DIM))
    heads_b = lambda t: t.reshape(t.shape[:-1] + (N_HEADS_B, HEAD_DIM))
    ka_p, va_p = heads_a(kva_p[..., :LANES]), heads_a(kva_p[..., LANES:])
    kb_p, vb_p = heads_b(jnp.stack(kb_p)), heads_b(jnp.stack(vb_p))
    ka_s = heads_a(kva_s[..., :LANES].reshape(DEPTH, DEC_BATCH, DEC_SEQ, LANES))
    va_s = heads_a(kva_s[..., LANES:].reshape(DEPTH, DEC_BATCH, DEC_SEQ, LANES))
    keep_a = min(WINDOW_A, la + DEC_SEQ)
    new_ak = jnp.concatenate([cache_a_k, ka_s], axis=2)[:, :, -keep_a:]
    new_av = jnp.concatenate([cache_a_v, va_s], axis=2)[:, :, -keep_a:]
    new_bk = from_t(_cache_tail_call(cbk_t, shifted_caches[0], jnp.stack(kb_s)))
    new_bv = from_t(_cache_tail_call(cbv_t, shifted_caches[1], jnp.stack(vb_s)))
    return (y_prompt, y_sample, ka_p, va_p, kb_p, vb_p, new_ak, new_av, new_bk, new_bv)
```

```python
import functools
import math

import jax
import jax.numpy as jnp
from jax import lax
from jax.experimental import pallas as pl
from jax.experimental.pallas import tpu as pltpu

F32 = jnp.float32
BF16 = jnp.bfloat16

D_MODEL = 2048
BATCH = 2
SEQ = 4096
DEPTH = 4
DEC_BATCH = 32
DEC_SEQ = 8
PAST_LEN = 16384
HEAD_DIM = 64
N_HEADS_A = 16
N_KV_A = 2
N_HEADS_B = 16
WINDOW_A = 128
DILATED = ((128, 1), (512, 4), (2048, 16))
WINDOW_B = 2048
BLOCK = 128
NUM_BUCKETS = 32
MAX_DISTANCE = 2048
ATTN_SCALE = HEAD_DIM ** -0.5
NEG_INF = -1e30
D_FF = 5632
N_EXPERTS = 8
TOP_K = 2
D_FF_EXPERT = 7168
RMS_EPS = 1e-5

N_PROMPT = BATCH * SEQ
N_SAMPLE = DEC_BATCH * DEC_SEQ
N_TOK = N_PROMPT + N_SAMPLE
WIDTH_A = N_HEADS_A * HEAD_DIM
WIDTH_KVA = 2 * N_KV_A * HEAD_DIM
WIDTH_B = N_HEADS_B * HEAD_DIM
LANES = 128
ROW_CHUNKS = D_MODEL // LANES
MOE_ROWS = 256
N_ASSIGN = TOP_K * N_TOK
N_MOE_BLOCKS = N_ASSIGN // MOE_ROWS + N_EXPERTS
N_MOE_ROWS = N_MOE_BLOCKS * MOE_ROWS
VMEM_LIMIT = 56 * 1024 * 1024

TM = 768
CAST_ROWS = 256


def _params(sem):
    return pltpu.CompilerParams(dimension_semantics=sem, vmem_limit_bytes=VMEM_LIMIT)


def _cast_weight(w_ref, wbf_ref):
    steps = w_ref.shape[0] // CAST_ROWS

    def body(i, c):
        rows = pl.ds(pl.multiple_of(i * CAST_ROWS, CAST_ROWS), CAST_ROWS)
        wbf_ref[rows, :] = w_ref[rows, :].astype(BF16)
        return c

    lax.fori_loop(0, steps, body, 0)


def _rmsnorm_rows(x, g):
    ms = jnp.mean(x * x, axis=-1, keepdims=True)
    return x * lax.rsqrt(ms + RMS_EPS) * g


def _norm_kernel(x_ref, g_ref, o_ref):
    o_ref[...] = _rmsnorm_rows(x_ref[...], g_ref[...]).astype(o_ref.dtype)


def _norm_call(x, g, out_dtype, tm=256):
    rows, width = x.shape
    return pl.pallas_call(
        _norm_kernel,
        grid=(rows // tm,),
        in_specs=[pl.BlockSpec((tm, width), lambda i: (i, 0)),
                  pl.BlockSpec((1, width), lambda i: (0, 0))],
        out_specs=pl.BlockSpec((tm, width), lambda i: (i, 0)),
        out_shape=jax.ShapeDtypeStruct((rows, width), out_dtype),
        compiler_params=_params(("arbitrary",)),
        name="rmsnorm",
    )(x, g.reshape(1, width))


QKV_WIDTHS = (WIDTH_A, WIDTH_KVA, WIDTH_B, WIDTH_B, WIDTH_B)
QKV_DIM = sum(QKV_WIDTHS)
TM_QKV = 384
TN_QKV = 512


def _qkv_kernel(x_ref, g_ref, w_ref, *o_refs):
    xn = _rmsnorm_rows(x_ref[...], g_ref[...]).astype(BF16)
    col = 0
    for o_ref in o_refs:
        width = o_ref.shape[1]
        for c in range(0, width, TN_QKV):
            cols = min(TN_QKV, width - c)
            o_ref[:, c:c + cols] = jnp.dot(xn, w_ref[:, col + c:col + c + cols], preferred_element_type=F32)
        col += width


def _to_bf16_kernel(w_ref, o_ref):
    o_ref[...] = w_ref[...].astype(BF16)


def _to_bf16_call(w, layer, rows=256):
    k, n = w.shape[1:]
    return pl.pallas_call(
        _to_bf16_kernel,
        grid=(k // rows,),
        in_specs=[pl.BlockSpec((None, rows, n), lambda i: (layer, i, 0))],
        out_specs=pl.BlockSpec((rows, n), lambda i: (i, 0)),
        out_shape=jax.ShapeDtypeStruct((k, n), BF16),
        compiler_params=_params(("arbitrary",)),
        name="weight_to_bf16",
    )(w)


def _qkv_call(x, g, w_in, layer):
    w_bf = _to_bf16_call(w_in, layer)
    return pl.pallas_call(
        _qkv_kernel,
        grid=(N_TOK // TM_QKV,),
        in_specs=[pl.BlockSpec((TM_QKV, D_MODEL), lambda m: (m, 0)),
                  pl.BlockSpec((1, D_MODEL), lambda m: (0, 0)),
                  pl.BlockSpec((D_MODEL, QKV_DIM), lambda m: (0, 0), pipeline_mode=pl.Buffered(1))],
        out_specs=[pl.BlockSpec((TM_QKV, w), lambda m: (m, 0)) for w in QKV_WIDTHS],
        out_shape=[jax.ShapeDtypeStruct((N_TOK, w), F32) for w in QKV_WIDTHS],
        compiler_params=_params(("arbitrary",)),
        name="qkv_proj",
    )(x, g.reshape(1, D_MODEL), w_bf)


def _mm_res_kernel(a_ref, w_ref, r_ref, o_ref, wbf_ref):
    @pl.when(pl.program_id(1) == 0)
    def _():
        _cast_weight(w_ref, wbf_ref)

    o_ref[...] = r_ref[...] + jnp.dot(a_ref[...], wbf_ref[...], preferred_element_type=F32)


def _mm_res_call(a, w, layer, res, tm, tn, name):
    m_rows, k = a.shape
    n_cols = w.shape[2]
    return pl.pallas_call(
        _mm_res_kernel,
        grid=(n_cols // tn, m_rows // tm),
        in_specs=[pl.BlockSpec((tm, k), lambda n, m: (m, 0)),
                  pl.BlockSpec((None, k, tn), lambda n, m: (layer, 0, n)),
                  pl.BlockSpec((tm, tn), lambda n, m: (m, n))],
        out_specs=pl.BlockSpec((tm, tn), lambda n, m: (m, n)),
        out_shape=jax.ShapeDtypeStruct((m_rows, n_cols), F32),
        scratch_shapes=[pltpu.VMEM((k, tn), BF16)],
        compiler_params=_params(("arbitrary", "arbitrary")),
        name=name,
    )(a, w, res)


def _silu_mul(g, u):
    return g * (1.0 / (1.0 + jnp.exp(-g))) * u


def _swiglu_kernel(a_ref, wg_ref, wu_ref, o_ref, wgbf_ref, wubf_ref):
    @pl.when(pl.program_id(1) == 0)
    def _():
        _cast_weight(wg_ref, wgbf_ref)
        _cast_weight(wu_ref, wubf_ref)

    a = a_ref[...]
    g = jnp.dot(a, wgbf_ref[...], preferred_element_type=F32)
    u = jnp.dot(a, wubf_ref[...], preferred_element_type=F32)
    o_ref[...] = _silu_mul(g, u).astype(o_ref.dtype)


def _swiglu_call(a, wg, wu, idx, tm=TM, tn=512):
    m_rows, k = a.shape
    n_cols = wg.shape[2]
    wspec = pl.BlockSpec((None, k, tn), lambda n, m: (idx, 0, n))
    return pl.pallas_call(
        _swiglu_kernel,
        grid=(n_cols // tn, m_rows // tm),
        in_specs=[pl.BlockSpec((tm, k), lambda n, m: (m, 0)), wspec, wspec],
        out_specs=pl.BlockSpec((tm, tn), lambda n, m: (m, n)),
        out_shape=jax.ShapeDtypeStruct((m_rows, n_cols), BF16),
        scratch_shapes=[pltpu.VMEM((k, tn), BF16), pltpu.VMEM((k, tn), BF16)],
        compiler_params=_params(("arbitrary", "arbitrary")),
        name="dense_gate_up",
    )(a, wg, wu)


def _t5_bucket_idx(dist):
    n = jnp.maximum(dist, 0)
    max_exact = NUM_BUCKETS // 2
    ratio = jnp.log(jnp.maximum(n, max_exact).astype(F32) / max_exact) / math.log(MAX_DISTANCE / max_exact)
    large = max_exact + (ratio * (NUM_BUCKETS - max_exact)).astype(jnp.int32)
    large = jnp.minimum(large, NUM_BUCKETS - 1)
    return jnp.where(n < max_exact, n, large)


def _bias_kernel(idx_ref, tab_ref, o_ref):
    idx = idx_ref[...]
    acc = jnp.full(idx.shape, NEG_INF, F32)
    for b in range(NUM_BUCKETS):
        acc = jnp.where(idx == b, tab_ref[b:b + 1, :], acc)
    o_ref[...] = acc


def _bias_call(idx, tab_rows):
    groups = tab_rows.shape[0]
    rows, cols = idx.shape
    return pl.pallas_call(
        _bias_kernel,
        grid=(groups,),
        in_specs=[pl.BlockSpec((rows, cols), lambda g: (0, 0)),
                  pl.BlockSpec((None, NUM_BUCKETS, cols), lambda g: (g, 0, 0))],
        out_specs=pl.BlockSpec((None, rows, cols), lambda g: (g, 0, 0)),
        out_shape=jax.ShapeDtypeStruct((groups, rows, cols), F32),
        compiler_params=_params(("arbitrary",)),
        name="rel_bias",
    )(idx, tab_rows)


def _band_bucket_idx(step, max_dist):
    i = lax.broadcasted_iota(jnp.int32, (BLOCK, 2 * BLOCK), 0)
    c = lax.broadcasted_iota(jnp.int32, (BLOCK, 2 * BLOCK), 1)
    dist = (i - c + BLOCK) * step
    valid = (dist >= 0) & (dist <= max_dist)
    return jnp.where(valid, _t5_bucket_idx(dist), -1)


def _dilation_multiplicity(dist):
    mult = jnp.zeros(dist.shape, jnp.int32)
    for w, r in DILATED:
        mult = mult + ((dist >= 0) & (dist <= w) & (dist % r == 0)).astype(jnp.int32)
    return mult


def _lane_half_mask():
    return lax.broadcasted_iota(jnp.int32, (BLOCK, LANES), 1) < HEAD_DIM


def _dup_head(t, kv):
    rolled = pltpu.roll(t, HEAD_DIM, axis=1)
    low = lax.broadcasted_iota(jnp.int32, t.shape, 1) < HEAD_DIM
    return jnp.where(low, t, rolled) if kv == 0 else jnp.where(low, rolled, t)


def _nt_dot(a, b):
    return lax.dot_general(a, b, (((1,), (1,)), ((), ())), preferred_element_type=F32)


def _band_scores(q2, kp, kc, bias, heads, first):
    low = _lane_half_mask()
    out = []
    for half, h in enumerate(heads):
        qh = jnp.where(low if half == 0 else ~low, q2, 0.0).astype(BF16)
        bias_prev = bias[h, :, 0:BLOCK]
        if first is not False:
            bias_prev = jnp.where(first, NEG_INF, bias_prev)
        out.append((_nt_dot(qh, kp) + bias_prev, _nt_dot(qh, kc) + bias[h, :, BLOCK:2 * BLOCK]))
    return out


def _band_weights(scores, heads, sink_ref):
    out = []
    for (sp, sc), h in zip(scores, heads):
        m = jnp.max(jnp.maximum(sp, sc), axis=1, keepdims=True)
        if sink_ref is not None:
            m = jnp.maximum(m, sink_ref[h])
        pp = jnp.exp(sp - m)
        pc = jnp.exp(sc - m)
        l = jnp.sum(pp + pc, axis=1, keepdims=True)
        if sink_ref is not None:
            l = l + jnp.exp(sink_ref[h] - m)
        out.append((m, l, pp.astype(BF16), pc.astype(BF16)))
    return out


def _band_values(weights, vp, vc):
    low = _lane_half_mask()
    os_ = [jnp.dot(pp, vp, preferred_element_type=F32) + jnp.dot(pc, vc, preferred_element_type=F32)
           for _, _, pp, pc in weights]
    pick = lambda a, b: jnp.where(low, a, b)
    return pick(weights[0][0], weights[1][0]), pick(weights[0][1], weights[1][1]), pick(os_[0], os_[1])


SCORE_LOOKAHEAD = 2


def _pipelined_pairs(n_items, scores_of, values_of, finish, sink_ref=None):
    pending = {}
    for i in range(n_items + SCORE_LOOKAHEAD):
        if i < n_items:
            q2, kp, kc, bias, heads, first = scores_of(i)
            pending[i] = (heads, _band_scores(q2, kp, kc, bias, heads, first))
        j = i - SCORE_LOOKAHEAD
        if j >= 0:
            heads, scores = pending.pop(j)
            finish(j, *_band_values(_band_weights(scores, heads, sink_ref), *values_of(j)))


def _swa_kernel(sink_ref, q_ref, kvp_ref, kvc_ref, bm_ref, o_ref):
    first = pl.program_id(1) == 0
    dup = lambda ref, cols: [_dup_head(ref[:, cols], kv).astype(BF16) for kv in range(N_KV_A)]
    kprev, kcur = dup(kvp_ref, slice(0, LANES)), dup(kvc_ref, slice(0, LANES))
    vprev, vcur = dup(kvp_ref, slice(LANES, 2 * LANES)), dup(kvc_ref, slice(LANES, 2 * LANES))
    cols = lambda p: slice(p * LANES, (p + 1) * LANES)
    kv_of = lambda p: (2 * p) // (N_HEADS_A // N_KV_A)

    def scores_of(p):
        return q_ref[:, cols(p)] * ATTN_SCALE, kprev[kv_of(p)], kcur[kv_of(p)], bm_ref, (2 * p, 2 * p + 1), first

    def finish(p, m, l, o):
        o_ref[:, cols(p)] = o / l

    _pipelined_pairs(WIDTH_A // LANES, scores_of, lambda p: (vprev[kv_of(p)], vcur[kv_of(p)]), finish, sink_ref)


def _swa_prompt_call(qa, kva, bm, sink):
    nb = SEQ // BLOCK
    cur = lambda b, n: (b * nb + n, 0)
    prev = lambda b, n: (b * nb + jnp.maximum(n - 1, 0), 0)
    return pl.pallas_call(
        _swa_kernel,
        grid=(BATCH, nb),
        in_specs=[pl.BlockSpec(memory_space=pltpu.SMEM),
                  pl.BlockSpec((BLOCK, WIDTH_A), cur),
                  pl.BlockSpec((BLOCK, WIDTH_KVA), prev),
                  pl.BlockSpec((BLOCK, WIDTH_KVA), cur),
                  pl.BlockSpec((N_HEADS_A, BLOCK, 2 * BLOCK), lambda b, n: (0, 0, 0))],
        out_specs=pl.BlockSpec((BLOCK, WIDTH_A), cur),
        out_shape=jax.ShapeDtypeStruct((N_PROMPT, WIDTH_A), F32),
        compiler_params=_params(("arbitrary", "arbitrary")),
        name="swa_prompt",
    )(sink, qa, kva, kva, bm)


SUPER = BLOCK * DILATED[-1][1]
HEADS_PER_STEP = 2
STEP_WIDTH = HEADS_PER_STEP * HEAD_DIM
BLOCKS_PER_STEP = 4
SUBLANES = 8


def _dilated_kernel(q_ref, k_ref, v_ref, bm_ref, o_ref, kwin, vwin, m_s, l_s, acc_s):
    n = pl.program_id(2)

    @pl.when(n == 0)
    def _():
        kwin[0:SUPER, :] = jnp.zeros((SUPER, STEP_WIDTH), F32)
        vwin[0:SUPER, :] = jnp.zeros((SUPER, STEP_WIDTH), F32)

    @pl.when(n > 0)
    def _():
        kwin[0:SUPER, :] = kwin[SUPER:2 * SUPER, :]
        vwin[0:SUPER, :] = vwin[SUPER:2 * SUPER, :]

    kwin[SUPER:2 * SUPER, :] = k_ref[...]
    vwin[SUPER:2 * SUPER, :] = v_ref[...]

    def sweep(d, r, starts, firsts):
        last = d == len(DILATED) - 1
        rows = lambda s: pl.ds(s, BLOCK, stride=r) if r > 1 else pl.ds(s, BLOCK)

        def scores_of(i):
            q2 = q_ref[rows(starts[i]), :] * ATTN_SCALE
            kp = kwin[rows(SUPER + starts[i] - BLOCK * r), :].astype(BF16)
            kc = kwin[rows(SUPER + starts[i]), :].astype(BF16)
            return q2, kp, kc, bm_ref.at[d], (0, 1), firsts[i]

        def values_of(i):
            return (vwin[rows(SUPER + starts[i] - BLOCK * r), :].astype(BF16),
                    vwin[rows(SUPER + starts[i]), :].astype(BF16))

        def finish(i, m, l, o):
            at = rows(starts[i])
            if d > 0:
                m_old = m_s[at, :]
                m_new = jnp.maximum(m_old, m)
                a_old = jnp.exp(m_old - m_new)
                a_new = jnp.exp(m - m_new)
                l = a_old * l_s[at, :] + a_new * l
                o = a_old * acc_s[at, :] + a_new * o
                m = m_new
            if last:
                o_ref[at, :] = o / l
            else:
                m_s[at, :] = m
                l_s[at, :] = l
                acc_s[at, :] = o

        _pipelined_pairs(len(starts), scores_of, values_of, finish)

    for d, (_, r) in enumerate(DILATED):
        span = BLOCK * r
        per_step = max(BLOCKS_PER_STEP, min(r, SUBLANES))
        if r >= per_step:
            groups = r // per_step
            assert groups == 1 or per_step % SUBLANES == 0
            offsets = tuple(range(per_step))
            base_of = lambda i, groups=groups, span=span, per_step=per_step: (
                pl.multiple_of((i // groups) * span + (i % groups) * per_step, SUBLANES))
            firsts_of = lambda i, groups=groups, per_step=per_step: [(n == 0) & (i // groups == 0)] * per_step
            steps = (SUPER // span) * groups
        else:
            offsets = tuple(k * span for k in range(per_step))
            base_of = lambda i, span=span, per_step=per_step: pl.multiple_of(i * span * per_step, SUBLANES)
            firsts_of = lambda i, per_step=per_step: [(n == 0) & (i == 0)] + [False] * (per_step - 1)
            steps = SUPER // (span * per_step)

        def body(i, c, d=d, r=r, offsets=offsets, base_of=base_of, firsts_of=firsts_of):
            base = base_of(i)
            sweep(d, r, [base + off for off in offsets], firsts_of(i))
            return c

        lax.fori_loop(0, steps, body, 0)


def _dilated_prompt_call(qb, kb, vb, bm):
    tiles = SEQ // SUPER
    tile = pl.BlockSpec((SUPER, STEP_WIDTH), lambda b, g, n: (b * tiles + n, g))
    return pl.pallas_call(
        _dilated_kernel,
        grid=(BATCH, N_HEADS_B // HEADS_PER_STEP, tiles),
        in_specs=[tile, tile, tile,
                  pl.BlockSpec((len(DILATED), HEADS_PER_STEP, BLOCK, 2 * BLOCK), lambda b, g, n: (0, g, 0, 0))],
        out_specs=tile,
        out_shape=jax.ShapeDtypeStruct((N_PROMPT, WIDTH_B), F32),
        scratch_shapes=[pltpu.VMEM((2 * SUPER, STEP_WIDTH), F32), pltpu.VMEM((2 * SUPER, STEP_WIDTH), F32),
                        pltpu.VMEM((SUPER, STEP_WIDTH), F32), pltpu.VMEM((SUPER, STEP_WIDTH), F32),
                        pltpu.VMEM((SUPER, STEP_WIDTH), F32)],
        compiler_params=_params(("arbitrary", "arbitrary", "arbitrary")),
        name="dilated_prompt",
    )(qb, kb, vb, bm)


TM_OUT = 256


def _out_proj_kernel(oa_ref, ob_ref, ys_ref, ga_ref, gb_ref, w_ref, x_ref, o_ref, wbf_ref):
    m = pl.program_id(0)

    @pl.when(m == 0)
    def _():
        _cast_weight(w_ref, wbf_ref)

    def project(ya, yb):
        o_ref[...] = (x_ref[...]
                      + jnp.dot(ya.astype(BF16), wbf_ref[0:WIDTH_A, :], preferred_element_type=F32)
                      + jnp.dot(yb.astype(BF16), wbf_ref[WIDTH_A:, :], preferred_element_type=F32))

    @pl.when(m < N_PROMPT // TM_OUT)
    def _():
        project(_rmsnorm_rows(oa_ref[...], ga_ref[...]), _rmsnorm_rows(ob_ref[...], gb_ref[...]))

    @pl.when(m >= N_PROMPT // TM_OUT)
    def _():
        project(ys_ref[:, 0:WIDTH_A], ys_ref[:, WIDTH_A:])


def _out_proj_call(oa, ob, y_sample, ga, gb, w_out, layer, x):
    assert N_SAMPLE == TM_OUT
    last_prompt = N_PROMPT // TM_OUT - 1
    prompt = lambda w: pl.BlockSpec((TM_OUT, w), lambda m: (jnp.minimum(m, last_prompt), 0))
    const = lambda *shape: pl.BlockSpec(shape, lambda m: (0,) * len(shape))
    rows = pl.BlockSpec((TM_OUT, D_MODEL), lambda m: (m, 0))
    width = WIDTH_A + WIDTH_B
    return pl.pallas_call(
        _out_proj_kernel,
        grid=(N_TOK // TM_OUT,),
        in_specs=[prompt(WIDTH_A), prompt(WIDTH_B), const(N_SAMPLE, width), const(1, WIDTH_A), const(1, WIDTH_B),
                  pl.BlockSpec((None, width, D_MODEL), lambda m: (layer, 0, 0), pipeline_mode=pl.Buffered(1)),
                  rows],
        out_specs=rows,
        out_shape=jax.ShapeDtypeStruct((N_TOK, D_MODEL), F32),
        scratch_shapes=[pltpu.VMEM((width, D_MODEL), BF16)],
        compiler_params=_params(("arbitrary",)),
        name="out_proj",
    )(oa, ob, y_sample, ga.reshape(1, -1), gb.reshape(1, -1), w_out, x)


NEW_ROWS = 128
SINK_ROW = DEC_SEQ
HEAD_GROUPS = 2
GROUP_WIDTH = WIDTH_B // HEAD_GROUPS


def _pad_rows(t, rows):
    return jnp.concatenate([t, jnp.zeros((rows - t.shape[0], t.shape[1]), t.dtype)], axis=0)


def _sample_window_a(qa_ref, kva_ref, cak_ref, cav_ref, bac_ref, ban_ref, sink_ref):
    group = N_HEADS_A // N_KV_A
    lane8 = lax.broadcasted_iota(jnp.int32, (DEC_SEQ, LANES), 1) < HEAD_DIM
    rows = []
    for p in range(WIDTH_A // LANES):
        blk = qa_ref[:, p * LANES:(p + 1) * LANES] * ATTN_SCALE
        rolled = pltpu.roll(blk, HEAD_DIM, axis=1)
        if (2 * p) // group == 0:
            rows += [jnp.where(lane8, blk, 0.0), jnp.where(lane8, rolled, 0.0)]
        else:
            rows += [jnp.where(lane8, 0.0, rolled), jnp.where(lane8, 0.0, blk)]
    qat = jnp.concatenate(rows, axis=0).astype(BF16)
    kan = _pad_rows(kva_ref[:, 0:LANES], NEW_ROWS)
    van = _pad_rows(kva_ref[:, LANES:2 * LANES], NEW_ROWS)
    row_id = lax.broadcasted_iota(jnp.int32, (NEW_ROWS, LANES), 0)
    sc = _nt_dot(cak_ref[...].astype(BF16), qat) + bac_ref[...]
    sn = _nt_dot(kan.astype(BF16), qat) + jnp.where(row_id == SINK_ROW, sink_ref[...], ban_ref[...])
    m = jnp.maximum(jnp.max(sc, axis=0, keepdims=True), jnp.max(sn, axis=0, keepdims=True))
    pct = jnp.exp(sc - m).T
    pnt = jnp.exp(sn - m).T
    l_col = jnp.sum(pct, axis=1, keepdims=True) + jnp.sum(pnt, axis=1, keepdims=True)
    pct = pct.astype(BF16)
    pnt = pnt.astype(BF16)
    res = []
    for kv in range(N_KV_A):
        vc = _dup_head(cav_ref[...], kv).astype(BF16)
        vn = _dup_head(van, kv).astype(BF16)
        res.append((jnp.dot(pct, vc, preferred_element_type=F32)
                    + jnp.dot(pnt, vn, preferred_element_type=F32)) / l_col)
    pairs = []
    for p in range(WIDTH_A // LANES):
        r = res[(2 * p) // group]
        pairs.append(jnp.where(lane8, r[16 * p:16 * p + 8, :], r[16 * p + 8:16 * p + 16, :]))
    return jnp.concatenate(pairs, axis=1)


KEEP_LANES = LANES - DEC_SEQ


def _shift_lanes(old_ref, out_ref, tail):
    feats, window = old_ref.shape
    lane = lax.broadcasted_iota(jnp.int32, (feats, LANES), 1)
    nxt = tail
    for j in reversed(range(window // LANES)):
        cur = pltpu.roll(old_ref[:, j * LANES:(j + 1) * LANES], KEEP_LANES, axis=1)
        out_ref[:, j * LANES:(j + 1) * LANES] = jnp.where(lane < KEEP_LANES, cur, nxt)
        nxt = cur


def _shift_in_new(old_ref, new_ref, out_ref):
    tails = []
    for c in range(old_ref.shape[0] // LANES):
        sq = _pad_rows(new_ref[:, c * LANES:(c + 1) * LANES], LANES).T
        tails.append(pltpu.roll(sq, KEEP_LANES, axis=1))
    _shift_lanes(old_ref, out_ref, jnp.concatenate(tails, axis=0))


def _sample_kernel(qa_ref, kva_ref, qb_ref, kb_ref, vb_ref, cak_ref, cav_ref, cbk_ref, cbv_ref,
                   bac_ref, ban_ref, sink_ref, bbc_ref, bbn_ref, mbc_ref, mbn_ref, ga_ref, gb_ref,
                   o_ref, oa_s, ob_s):
    g = pl.program_id(1)

    @pl.when(g == 0)
    def _():
        oa_s[...] = _sample_window_a(qa_ref, kva_ref, cak_ref, cav_ref, bac_ref, ban_ref, sink_ref)

    low = lax.broadcasted_iota(jnp.int32, (DEC_SEQ, LANES), 1) < HEAD_DIM
    n_pairs = GROUP_WIDTH // LANES
    cols = lambda p: slice(p * LANES, (p + 1) * LANES)

    def scores(p):
        q2 = qb_ref[:, cols(p)] * ATTN_SCALE
        kt = cbk_ref[cols(p), :].astype(BF16)
        kn = _pad_rows(kb_ref[:, cols(p)], NEW_ROWS).astype(BF16)
        out = []
        for half in range(2):
            qh = jnp.where(low if half == 0 else ~low, q2, 0.0).astype(BF16)
            out.append((jnp.dot(qh, kt, preferred_element_type=F32) + bbc_ref[2 * p + half],
                        _nt_dot(qh, kn) + bbn_ref[2 * p + half]))
        return out

    def values(p, pair_scores):
        vt = cbv_ref[cols(p), :].astype(BF16)
        vn = _pad_rows(vb_ref[:, cols(p)], NEW_ROWS).astype(BF16)
        halves = []
        for sc, sn in pair_scores:
            m = jnp.maximum(jnp.max(sc, axis=1, keepdims=True), jnp.max(sn, axis=1, keepdims=True))
            pc = jnp.exp(sc - m) * mbc_ref[...]
            pn = jnp.exp(sn - m) * mbn_ref[...]
            l = jnp.sum(pc, axis=1, keepdims=True) + jnp.sum(pn, axis=1, keepdims=True)
            o = _nt_dot(pc.astype(BF16), vt) + jnp.dot(pn.astype(BF16), vn, preferred_element_type=F32)
            halves.append(o / l)
        return jnp.where(low, halves[0], halves[1])

    pending, pairs = {}, []
    for p in range(n_pairs + SCORE_LOOKAHEAD):
        if p < n_pairs:
            pending[p] = scores(p)
        if p >= SCORE_LOOKAHEAD:
            pairs.append(values(p - SCORE_LOOKAHEAD, pending.pop(p - SCORE_LOOKAHEAD)))
    ob = jnp.concatenate(pairs, axis=1)
    for k in range(HEAD_GROUPS):
        @pl.when(g == k)
        def _(k=k):
            ob_s[:, k * GROUP_WIDTH:(k + 1) * GROUP_WIDTH] = ob

    @pl.when(g == HEAD_GROUPS - 1)
    def _():
        o_ref[:, 0:WIDTH_A] = _rmsnorm_rows(oa_s[...], ga_ref[...])
        o_ref[:, WIDTH_A:] = _rmsnorm_rows(ob_s[...], gb_ref[...])


def _sample_call(qa, kva, qb, kb, vb, cak, cav, cbk_t, cbv_t, consts, sink_row, ga, gb, layer):
    first = N_PROMPT // DEC_SEQ
    new = lambda w: pl.BlockSpec((DEC_SEQ, w), lambda b, g: (first + b, 0))
    new_g = pl.BlockSpec((DEC_SEQ, GROUP_WIDTH), lambda b, g: (first + b, g))
    la, lb = cak.shape[2], cbk_t.shape[3]
    cache_a = pl.BlockSpec((None, None, la, LANES), lambda b, g: (layer, b, 0, 0))
    cache_b = pl.BlockSpec((None, None, GROUP_WIDTH, lb), lambda b, g: (layer, b, g, 0))
    const = lambda *shape: pl.BlockSpec(shape, lambda b, g: (0,) * len(shape))
    heads = N_HEADS_B // HEAD_GROUPS
    per_head = lambda *shape: pl.BlockSpec((heads,) + shape, lambda b, g: (g,) + (0,) * len(shape))
    in_specs = [new(WIDTH_A), new(WIDTH_KVA), new_g, new_g, new_g, cache_a, cache_a, cache_b, cache_b,
                const(la, LANES), const(NEW_ROWS, LANES), const(1, LANES),
                per_head(DEC_SEQ, lb), per_head(DEC_SEQ, NEW_ROWS), const(DEC_SEQ, lb), const(DEC_SEQ, NEW_ROWS),
                const(1, WIDTH_A), const(1, WIDTH_B)]
    args = [qa, kva, qb, kb, vb, cak, cav, cbk_t, cbv_t,
            consts["bac"], consts["ban"], sink_row, consts["bbc"], consts["bbn"], consts["mbc"], consts["mbn"],
            ga.reshape(1, -1), gb.reshape(1, -1)]
    return pl.pallas_call(
        _sample_kernel,
        grid=(DEC_BATCH, HEAD_GROUPS),
        in_specs=in_specs,
        out_specs=pl.BlockSpec((DEC_SEQ, WIDTH_A + WIDTH_B), lambda b, g: (b, 0)),
        out_shape=jax.ShapeDtypeStruct((N_SAMPLE, WIDTH_A + WIDTH_B), F32),
        scratch_shapes=[pltpu.VMEM((DEC_SEQ, WIDTH_A), F32), pltpu.VMEM((DEC_SEQ, WIDTH_B), F32)],
        compiler_params=_params(("arbitrary", "arbitrary")),
        name="sample_attention",
    )(*args)


TAIL_ENTRIES = 8


def _cache_tail_kernel(old_ref, new_ref, shifted_hbm, out_ref):
    del shifted_hbm
    for i in range(TAIL_ENTRIES):
        _shift_in_new(old_ref.at[i], new_ref.at[pl.ds(i * DEC_SEQ, DEC_SEQ), :], out_ref.at[i])


def _cache_tail_call(cache_t, shifted, new_rows):
    last_tile = cache_t.shape[3] // LANES - 1
    tile = pl.BlockSpec((None, TAIL_ENTRIES, WIDTH_B, LANES), lambda l, b: (l, b, 0, last_tile))
    return pl.pallas_call(
        _cache_tail_kernel,
        grid=(DEPTH, DEC_BATCH // TAIL_ENTRIES),
        in_specs=[tile, pl.BlockSpec((None, TAIL_ENTRIES * DEC_SEQ, WIDTH_B), lambda l, b: (l, b, 0)),
                  pl.BlockSpec(memory_space=pl.ANY)],
        out_specs=tile,
        out_shape=jax.ShapeDtypeStruct(cache_t.shape, F32),
        input_output_aliases={2: 0},
        compiler_params=_params(("arbitrary", "arbitrary")),
        name="cache_tail",
    )(cache_t, new_rows, shifted)


def _sample_consts(rel_bias):
    la, lb = WINDOW_A, WINDOW_B

    def idx_of(dist, valid):
        return jnp.where(valid, _t5_bucket_idx(dist), -1)

    t = lax.broadcasted_iota(jnp.int32, (1, LANES), 1) % DEC_SEQ
    tab_a = jnp.repeat(rel_bias[:, :N_HEADS_A], DEC_SEQ, axis=1)[None]
    i_a = lax.broadcasted_iota(jnp.int32, (la, LANES), 0)
    dist = la + t - i_a
    bac = _bias_call(idx_of(dist, dist <= WINDOW_A - 1), tab_a)[0]
    j = lax.broadcasted_iota(jnp.int32, (NEW_ROWS, LANES), 0)
    dist_n = t - j
    ban = _bias_call(idx_of(dist_n, (dist_n >= 0) & (j < DEC_SEQ)), tab_a)[0]
    tab_b = rel_bias[:, N_HEADS_A:].T[:, :, None]
    tq = lax.broadcasted_iota(jnp.int32, (DEC_SEQ, lb), 0)
    ik = lax.broadcasted_iota(jnp.int32, (DEC_SEQ, lb), 1)
    dist_b = lb + tq - ik
    mult_c = _dilation_multiplicity(dist_b)
    bbc = _bias_call(idx_of(dist_b, mult_c > 0), jnp.broadcast_to(tab_b, (N_HEADS_B, NUM_BUCKETS, lb)))
    tn = lax.broadcasted_iota(jnp.int32, (DEC_SEQ, NEW_ROWS), 0)
    jn = lax.broadcasted_iota(jnp.int32, (DEC_SEQ, NEW_ROWS), 1)
    mult_n = jnp.where(jn < DEC_SEQ, _dilation_multiplicity(tn - jn), 0)
    bbn = _bias_call(idx_of(tn - jn, mult_n > 0), jnp.broadcast_to(tab_b, (N_HEADS_B, NUM_BUCKETS, NEW_ROWS)))
    return {"bac": bac, "ban": ban, "bbc": bbc, "bbn": bbn,
            "mbc": mult_c.astype(F32), "mbn": mult_n.astype(F32)}


TM_ROUTER = 256


def _norm_router_kernel(x_ref, g_ref, rt_ref, chunks_ref, idx_ref, gate_ref):
    xn = _rmsnorm_rows(x_ref[...], g_ref[...])
    for j in range(ROW_CHUNKS):
        chunks_ref[pl.ds(j, TM_ROUTER, stride=ROW_CHUNKS), :] = xn[:, j * LANES:(j + 1) * LANES]
    logits = lax.dot_general(rt_ref[...], xn, (((1,), (1,)), ((), ())),
                             precision=lax.Precision.HIGHEST, preferred_element_type=F32)
    e_id = lax.broadcasted_iota(jnp.int32, logits.shape, 0)
    m1 = jnp.max(logits, axis=0, keepdims=True)
    i1 = jnp.min(jnp.where(logits == m1, e_id, N_EXPERTS), axis=0, keepdims=True)
    rest = jnp.where(e_id == i1, -jnp.inf, logits)
    m2 = jnp.max(rest, axis=0, keepdims=True)
    i2 = jnp.min(jnp.where(rest == m2, e_id, N_EXPERTS), axis=0, keepdims=True)
    e2 = jnp.exp(m2 - m1)
    den = 1.0 + e2
    idx_ref[0:1, :] = i1
    idx_ref[1:2, :] = i2
    gate_ref[0:1, :] = 1.0 / den
    gate_ref[1:2, :] = e2 / den


def _norm_router_call(x, g, router_t):
    tm = TM_ROUTER
    return pl.pallas_call(
        _norm_router_kernel,
        grid=(N_TOK // tm,),
        in_specs=[pl.BlockSpec((tm, D_MODEL), lambda i: (i, 0)),
                  pl.BlockSpec((1, D_MODEL), lambda i: (0, 0)),
                  pl.BlockSpec((N_EXPERTS, D_MODEL), lambda i: (0, 0))],
        out_specs=[pl.BlockSpec((tm * ROW_CHUNKS, LANES), lambda i: (i, 0)),
                   pl.BlockSpec((TOP_K, tm), lambda i: (0, i)),
                   pl.BlockSpec((TOP_K, tm), lambda i: (0, i))],
        out_shape=[jax.ShapeDtypeStruct((N_TOK * ROW_CHUNKS, LANES), F32),
                   jax.ShapeDtypeStruct((TOP_K, N_TOK), jnp.int32),
                   jax.ShapeDtypeStruct((TOP_K, N_TOK), F32)],
        compiler_params=_params(("arbitrary",)),
        name="moe_norm_router",
    )(x, g.reshape(1, D_MODEL), router_t)


def _routing_plan(top_i):
    e = top_i.T.reshape(-1)
    sorted_tok = (jnp.argsort(e, stable=True) // TOP_K).astype(jnp.int32)
    onehot = (e[:, None] == jnp.arange(N_EXPERTS)[None, :]).astype(jnp.int32)
    counts = jnp.sum(onehot, axis=0)
    starts = jnp.cumsum(counts) - counts
    padded = (counts + MOE_ROWS - 1) // MOE_ROWS * MOE_ROWS
    pstarts = jnp.cumsum(padded) - padded
    blk_start = jnp.arange(N_MOE_BLOCKS) * MOE_ROWS
    blk_e = jnp.minimum(jnp.sum(blk_start[:, None] >= (pstarts + padded)[None, :], axis=1), N_EXPERTS - 1)
    n_used = (jnp.sum(padded) // MOE_ROWS).astype(jnp.int32).reshape(1)
    blk_onehot = (blk_e[:, None] == jnp.arange(N_EXPERTS)[None, :]).astype(jnp.int32)
    pick = lambda table, hot: jnp.sum(hot * table[None, :], axis=1)
    into = blk_start - pick(pstarts, blk_onehot)
    blk_src = (pick(starts, blk_onehot) + into).astype(jnp.int32)
    blk_cnt = jnp.clip(pick(counts, blk_onehot) - into, 0, MOE_ROWS).astype(jnp.int32)
    rank = jnp.sum(onehot * (jnp.cumsum(onehot, axis=0) - onehot), axis=1)
    dest = (pick(pstarts, onehot) + rank).astype(jnp.int32)
    dest_kt = dest.reshape(N_TOK, TOP_K).T.reshape(-1)
    return blk_e.astype(jnp.int32), n_used, (sorted_tok, blk_src, blk_cnt), dest_kt


def _row_copy(src_hbm, row, dst, slot, sem):
    return pltpu.make_async_copy(
        src_hbm.at[pl.ds(pl.multiple_of(row * ROW_CHUNKS, ROW_CHUNKS), ROW_CHUNKS), :],
        dst.at[pl.ds(pl.multiple_of(slot * ROW_CHUNKS, ROW_CHUNKS), ROW_CHUNKS), :],
        sem)


def _wait_rows(src_hbm, dst, sem):
    pltpu.make_async_copy(src_hbm.at[pl.ds(0, dst.shape[0]), :], dst, sem).wait()


ISSUE_UNROLL = 8


def _gather_kernel(tok_ref, src_ref, cnt_ref, x_hbm, o_ref, buf0, buf1, sems):
    b = pl.program_id(0)
    bufs = (buf0, buf1)

    def request(block, slot):
        first, count = src_ref[block], cnt_ref[block]

        def issue(i, c):
            for u in range(ISSUE_UNROLL):
                row = i * ISSUE_UNROLL + u
                tok = jnp.where(row < count, tok_ref[jnp.minimum(first + row, N_ASSIGN - 1)], 0)
                _row_copy(x_hbm, tok, bufs[slot], row, sems.at[slot]).start()
            return c

        lax.fori_loop(0, MOE_ROWS // ISSUE_UNROLL, issue, 0)

    @pl.when(b == 0)
    def _():
        request(0, 0)

    for slot in range(2):
        @pl.when((b + 1 < N_MOE_BLOCKS) & ((b + 1) % 2 == slot))
        def _(slot=slot):
            request(b + 1, slot)

    for slot in range(2):
        @pl.when(b % 2 == slot)
        def _(slot=slot):
            _wait_rows(x_hbm, bufs[slot], sems.at[slot])
            for j in range(ROW_CHUNKS):
                chunk = bufs[slot][pl.ds(j, MOE_ROWS, stride=ROW_CHUNKS), :]
                o_ref[:, j * LANES:(j + 1) * LANES] = chunk.astype(o_ref.dtype)


def _gather_call(rows, x_chunks):
    grid_spec = pltpu.PrefetchScalarGridSpec(
        num_scalar_prefetch=3,
        grid=(N_MOE_BLOCKS,),
        in_specs=[pl.BlockSpec(memory_space=pl.ANY)],
        out_specs=pl.BlockSpec((MOE_ROWS, D_MODEL), lambda b, tok, src, cnt: (b, 0)),
        scratch_shapes=[pltpu.VMEM((MOE_ROWS * ROW_CHUNKS, LANES), F32),
                        pltpu.VMEM((MOE_ROWS * ROW_CHUNKS, LANES), F32),
                        pltpu.SemaphoreType.DMA((2,))],
    )
    return pl.pallas_call(
        _gather_kernel,
        grid_spec=grid_spec,
        out_shape=jax.ShapeDtypeStruct((N_MOE_ROWS, D_MODEL), BF16),
        compiler_params=_params(("arbitrary",)),
        name="moe_dispatch",
    )(*rows, x_chunks)


def _expert_changed(be_ref, nu_ref):
    b = pl.program_id(1)
    last = jnp.minimum(b, nu_ref[0] - 1)
    return (b < nu_ref[0]) & ((b == 0) | (be_ref[last] != be_ref[jnp.maximum(last - 1, 0)]))


SHIFT_ROWS = 256
SHIFT_BLOCKS = DEPTH * DEC_BATCH * (WIDTH_B // SHIFT_ROWS)


def _moe_up_kernel(be_ref, nu_ref, a_ref, wg_ref, wu_ref, cache_ref, o_ref, shifted_ref, wgbf_ref, wubf_ref):
    @pl.when(_expert_changed(be_ref, nu_ref))
    def _():
        _cast_weight(wg_ref, wgbf_ref)
        _cast_weight(wu_ref, wubf_ref)

    used = pl.program_id(1) < nu_ref[0]
    shift = lambda: _shift_lanes(cache_ref, shifted_ref, jnp.zeros((SHIFT_ROWS, LANES), F32))

    @pl.when(used)
    def _():
        a = a_ref[...]
        g = jnp.dot(a, wgbf_ref[...], preferred_element_type=F32)
        u = jnp.dot(a, wubf_ref[...], preferred_element_type=F32)
        o_ref[...] = _silu_mul(g, u).astype(o_ref.dtype)
        shift()

    @pl.when(jnp.logical_not(used))
    def _():
        o_ref[...] = jnp.zeros(o_ref.shape, o_ref.dtype)
        shift()


def _moe_down_kernel(be_ref, nu_ref, a_ref, w_ref, o_ref, wbf_ref):
    @pl.when(_expert_changed(be_ref, nu_ref))
    def _():
        _cast_weight(w_ref, wbf_ref)

    used = pl.program_id(1) < nu_ref[0]

    @pl.when(used)
    def _():
        o_ref[...] = jnp.dot(a_ref[...], wbf_ref[...], preferred_element_type=F32)

    @pl.when(jnp.logical_not(used))
    def _():
        o_ref[...] = jnp.zeros(o_ref.shape, o_ref.dtype)


def _moe_specs(k, tn, idx):
    last = lambda b, nu: jnp.minimum(b, nu[0] - 1)
    a_spec = pl.BlockSpec((MOE_ROWS, k), lambda n, b, be, nu: (last(b, nu), 0))
    w_spec = pl.BlockSpec((None, None, k, tn), lambda n, b, be, nu: (idx, be[last(b, nu)], 0, n))
    o_spec = pl.BlockSpec((MOE_ROWS, tn), lambda n, b, be, nu: (b, n))
    return a_spec, w_spec, o_spec


def _moe_up_call(blk_e, n_used, a, wg, wu, idx, cache_t, tn=1024):
    k, n_cols = a.shape[1], wg.shape[3]
    a_spec, w_spec, o_spec = _moe_specs(k, tn, idx)
    assert (n_cols // tn) * N_MOE_BLOCKS >= SHIFT_BLOCKS
    per_entry = WIDTH_B // SHIFT_ROWS

    def cache_block(n, b, be, nu):
        s = jnp.minimum(n * N_MOE_BLOCKS + b, SHIFT_BLOCKS - 1)
        return (s // (DEC_BATCH * per_entry), (s // per_entry) % DEC_BATCH, s % per_entry, 0)

    cache_spec = pl.BlockSpec((None, None, SHIFT_ROWS, cache_t.shape[3]), cache_block)
    grid_spec = pltpu.PrefetchScalarGridSpec(
        num_scalar_prefetch=2,
        grid=(n_cols // tn, N_MOE_BLOCKS),
        in_specs=[a_spec, w_spec, w_spec, cache_spec],
        out_specs=[o_spec, cache_spec],
        scratch_shapes=[pltpu.VMEM((k, tn), BF16)] * 2,
    )
    return pl.pallas_call(
        _moe_up_kernel,
        grid_spec=grid_spec,
        out_shape=[jax.ShapeDtypeStruct((N_MOE_ROWS, n_cols), BF16), jax.ShapeDtypeStruct(cache_t.shape, F32)],
        compiler_params=_params(("arbitrary", "arbitrary")),
        name="moe_gate_up",
    )(blk_e, n_used, a, wg, wu, cache_t)


def _moe_down_call(blk_e, n_used, a, wd, idx, tn=512):
    k, n_cols = a.shape[1], wd.shape[3]
    a_spec, w_spec, o_spec = _moe_specs(k, tn, idx)
    grid_spec = pltpu.PrefetchScalarGridSpec(
        num_scalar_prefetch=2,
        grid=(n_cols // tn, N_MOE_BLOCKS),
        in_specs=[a_spec, w_spec],
        out_specs=o_spec,
        scratch_shapes=[pltpu.VMEM((k, tn), BF16)],
    )
    return pl.pallas_call(
        _moe_down_kernel,
        grid_spec=grid_spec,
        out_shape=jax.ShapeDtypeStruct((N_MOE_ROWS, n_cols), F32),
        compiler_params=_params(("arbitrary", "arbitrary")),
        name="moe_down",
    )(blk_e, n_used, a, wd)


COMBINE_ROWS = 256


def _combine_kernel(dest_ref, x_ref, g_ref, y_hbm, o_ref, *scratch):
    step = pl.program_id(0)
    sems = scratch[-1]
    bufs = (scratch[0:TOP_K], scratch[TOP_K:2 * TOP_K])

    def request(at, slot):
        def issue(i, c):
            for u in range(ISSUE_UNROLL):
                row = i * ISSUE_UNROLL + u
                for k in range(TOP_K):
                    src = dest_ref[k * N_TOK + at * COMBINE_ROWS + row]
                    _row_copy(y_hbm, src, bufs[slot][k], row, sems.at[slot * TOP_K + k]).start()
            return c

        lax.fori_loop(0, COMBINE_ROWS // ISSUE_UNROLL, issue, 0)

    @pl.when(step == 0)
    def _():
        request(0, 0)

    for slot in range(2):
        @pl.when((step + 1 < N_TOK // COMBINE_ROWS) & ((step + 1) % 2 == slot))
        def _(slot=slot):
            request(step + 1, slot)

    for slot in range(2):
        @pl.when(step % 2 == slot)
        def _(slot=slot):
            for k in range(TOP_K):
                _wait_rows(y_hbm, bufs[slot][k], sems.at[slot * TOP_K + k])
            g0 = g_ref[:, 0:1]
            g1 = g_ref[:, 1:2]
            for j in range(ROW_CHUNKS):
                cols = slice(j * LANES, (j + 1) * LANES)
                rows = pl.ds(j, COMBINE_ROWS, stride=ROW_CHUNKS)
                o_ref[:, cols] = x_ref[:, cols] + (g0 * bufs[slot][0][rows, :] + g1 * bufs[slot][1][rows, :])


def _combine_call(dest_kt, x, gates, y_chunks):
    grid_spec = pltpu.PrefetchScalarGridSpec(
        num_scalar_prefetch=1,
        grid=(N_TOK // COMBINE_ROWS,),
        in_specs=[pl.BlockSpec((COMBINE_ROWS, D_MODEL), lambda i, d: (i, 0)),
                  pl.BlockSpec((COMBINE_ROWS, TOP_K), lambda i, d: (i, 0)),
                  pl.BlockSpec(memory_space=pl.ANY)],
        out_specs=pl.BlockSpec((COMBINE_ROWS, D_MODEL), lambda i, d: (i, 0)),
        scratch_shapes=[pltpu.VMEM((COMBINE_ROWS * ROW_CHUNKS, LANES), F32)] * (2 * TOP_K)
        + [pltpu.SemaphoreType.DMA((2 * TOP_K,))],
    )
    return pl.pallas_call(
        _combine_kernel,
        grid_spec=grid_spec,
        out_shape=jax.ShapeDtypeStruct((N_TOK, D_MODEL), F32),
        compiler_params=_params(("arbitrary",)),
        name="moe_combine",
    )(dest_kt, x, gates, y_chunks)


def _moe_layer(x, g, router, wg, wu, wd, idx, cache_t):
    xn_chunks, top_i, gates = _norm_router_call(x, g, router[idx].T)
    blk_e, n_used, rows, dest_kt = _routing_plan(top_i)
    xs = _gather_call(rows, xn_chunks)
    h, shifted = _moe_up_call(blk_e, n_used, xs, wg, wu, idx, cache_t)
    y = _moe_down_call(blk_e, n_used, h, wd, idx)
    return _combine_call(dest_kt, x, gates.T, y.reshape(N_MOE_ROWS * ROW_CHUNKS, LANES)), shifted


def kernel(x_prompt, x_sample, cache_a_k, cache_a_v, cache_b_k, cache_b_v, g_mix_in, w_in, sinks, rel_bias,
           g_out_a, g_out_b, w_out, g_ffn, w_gate_d, w_up_d, w_down_d, router, w_gate_e, w_up_e, w_down_e,
           g_final):
    la, lb = cache_a_k.shape[2], cache_b_k.shape[2]
    x = jnp.concatenate([x_prompt.reshape(N_PROMPT, D_MODEL), x_sample.reshape(N_SAMPLE, D_MODEL)], axis=0)
    cak = cache_a_k.reshape(DEPTH, DEC_BATCH, la, N_KV_A * HEAD_DIM)
    cav = cache_a_v.reshape(DEPTH, DEC_BATCH, la, N_KV_A * HEAD_DIM)
    to_t = lambda c: jnp.transpose(c, (0, 1, 3, 4, 2)).reshape(DEPTH, DEC_BATCH, WIDTH_B, lb)
    from_t = lambda c: jnp.transpose(c.reshape(DEPTH, DEC_BATCH, N_HEADS_B, HEAD_DIM, lb), (0, 1, 4, 2, 3))
    cbk_t, cbv_t = to_t(cache_b_k), to_t(cache_b_v)

    head_rows = lambda tab: jnp.broadcast_to(tab.T[:, :, None], (tab.shape[1], NUM_BUCKETS, 2 * BLOCK))
    tab_a, tab_b = rel_bias[:, :N_HEADS_A], rel_bias[:, N_HEADS_A:]
    bm_a = _bias_call(_band_bucket_idx(1, WINDOW_A - 1), head_rows(tab_a))
    bm_b = jnp.stack([_bias_call(_band_bucket_idx(r, w), head_rows(tab_b)) for w, r in DILATED])
    consts = _sample_consts(rel_bias)

    tail = lambda t, keep: jnp.stack([t[(b + 1) * SEQ - keep:(b + 1) * SEQ] for b in range(BATCH)])
    kva_p, kb_p, vb_p, kva_s, kb_s, vb_s, shifted_caches = [], [], [], [], [], [], []
    for l in range(DEPTH):
        qa, kva, qb, kb, vb = _qkv_call(x, g_mix_in[l], w_in, l)
        oa = _swa_prompt_call(qa, kva, bm_a, sinks[l])
        ob = _dilated_prompt_call(qb, kb, vb, bm_b)
        sink_row = jnp.repeat(sinks[l], DEC_SEQ).reshape(1, LANES)
        y_sample = _sample_call(qa, kva, qb, kb, vb, cak, cav, cbk_t, cbv_t, consts, sink_row,
                                g_out_a[l], g_out_b[l], l)
        x = _out_proj_call(oa, ob, y_sample, g_out_a[l], g_out_b[l], w_out, l, x)
        if l % 2 == 0:
            hn = _norm_call(x, g_ffn[l], BF16)
            h = _swiglu_call(hn, w_gate_d, w_up_d, l // 2)
            x = _mm_res_call(h, w_down_d, l // 2, x, 384, 512, "dense_down")
        else:
            x, shifted = _moe_layer(x, g_ffn[l], router, w_gate_e, w_up_e, w_down_e, l // 2,
                                    (cbk_t, cbv_t)[(l // 2) % 2])
            shifted_caches.append(shifted)
        kva_p.append(tail(kva, min(WINDOW_A, SEQ)))
        kb_p.append(tail(kb, min(WINDOW_B, SEQ)))
        vb_p.append(tail(vb, min(WINDOW_B, SEQ)))
        kva_s.append(kva[N_PROMPT:])
        kb_s.append(kb[N_PROMPT:])
        vb_s.append(vb[N_PROMPT:])

    y = _norm_call(x, g_final, F32)
    y_prompt = y[:N_PROMPT].reshape(BATCH, SEQ, D_MODEL)
    y_sample = y[N_PROMPT:].reshape(DEC_BATCH, DEC_SEQ, D_MODEL)

    kva_p, kva_s = jnp.stack(kva_p), jnp.stack(kva_s)
    heads_a = lambda t: t.reshape(t.shape[:-1] + (N_KV_A, HEAD_DIM))
    heads_b = lambda t: t.reshape(t.shape[:-1] + (N_HEADS_B, HEAD_DIM))
    ka_p, va_p = heads_a(kva_p[..., :LANES]), heads_a(kva_p[..., LANES:])
    kb_p, vb_p = heads_b(jnp.stack(kb_p)), heads_b(jnp.stack(vb_p))
    ka_s = heads_a(kva_s[..., :LANES].reshape(DEPTH, DEC_BATCH, DEC_SEQ, LANES))
    va_s = heads_a(kva_s[..., LANES:].reshape(DEPTH, DEC_BATCH, DEC_SEQ, LANES))
    keep_a = min(WINDOW_A, la + DEC_SEQ)
    new_ak = jnp.concatenate([cache_a_k, ka_s], axis=2)[:, :, -keep_a:]
    new_av = jnp.concatenate([cache_a_v, va_s], axis=2)[:, :, -keep_a:]
    new_bk = from_t(_cache_tail_call(cbk_t, shifted_caches[0], jnp.stack(kb_s)))
    new_bv = from_t(_cache_tail_call(cbv_t, shifted_caches[1], jnp.stack(vb_s)))
    return (y_prompt, y_sample, ka_p, va_p, kb_p, vb_p, new_ak, new_av, new_bk, new_bv)
```
